```python
import jax
import jax.numpy as jnp
from jax import lax
import numpy as np

D_MODEL = 1024
BATCH = 4
SEQ = 4096
DEPTH = 4

GRID_W = 64
CTX_LEN = 256

BRANCH_WIDTH = 512
NA_HEADS = 8
NA_HEAD_DIM = 64
NA_WIDTH = NA_HEADS * NA_HEAD_DIM
NA_WIN_H = 8
NA_WIN_W = 16
RWKV_HEADS = 8
RWKV_HEAD_DIM = 64
RWKV_WIDTH = RWKV_HEADS * RWKV_HEAD_DIM
DECAY_LORA = 64
AAA_LORA = 64
GATE_LORA = 128
RWKV_IN = 3 * RWKV_WIDTH + 2 * DECAY_LORA + 2 * AAA_LORA + GATE_LORA
RWKV_GN_EPS = 64e-5
SGU_GROUPS = 8
SGU_WIDTH = 512
SGU_CHUNK = 128
SGU_IN = 2 * SGU_WIDTH
N_BRANCH = 3
D_FF = -(-8 * D_MODEL // (3 * 256)) * 256
LN_EPS = 1e-5
ALPHA = (2 * DEPTH) ** 0.25
BETA = (8 * DEPTH) ** -0.25

_IN_SIZES = (N_BRANCH * D_MODEL, NA_WIDTH, NA_WIDTH, NA_WIDTH, RWKV_IN, SGU_IN)
IN_SPLITS = tuple(int(s) for s in np.cumsum(_IN_SIZES)[:-1])
D_IN = int(sum(_IN_SIZES))
_RWKV_SIZES = (RWKV_WIDTH, RWKV_WIDTH, RWKV_WIDTH, 2 * DECAY_LORA, 2 * AAA_LORA, GATE_LORA)
RWKV_SPLITS = tuple(int(s) for s in np.cumsum(_RWKV_SIZES)[:-1])

kernel_name = 'hybrid_na_rwkv7_gmlp_deepnorm_block'


def layer_norm(x, g, b, eps=LN_EPS):
    xf = x.astype(jnp.float32)
    mu = jnp.mean(xf, axis=-1, keepdims=True)
    var = jnp.mean(jnp.square(xf - mu), axis=-1, keepdims=True)
    return ((xf - mu) * lax.rsqrt(var + eps) * g + b).astype(x.dtype)


def _heads(t):
    return t.reshape(t.shape[0], t.shape[1], NA_HEADS, NA_HEAD_DIM)


def neighbourhood_attention(q, k, v, kc, vc, rpb):
    B, S, H, Dh = q.shape
    rows = S // GRID_W
    wh = min(NA_WIN_H, rows)
    n_loc = wh * NA_WIN_W
    scale = Dh ** -0.5
    qg = q.reshape(B, rows, GRID_W, H, Dh)
    kg = k.reshape(B, rows, GRID_W, H, Dh)
    vg = v.reshape(B, rows, GRID_W, H, Dh)
    row0 = jnp.clip(jnp.arange(rows) - wh // 2, 0, rows - wh)
    cols = jnp.arange(GRID_W)
    key_cols = jnp.clip(cols - NA_WIN_W // 2, 0, GRID_W - NA_WIN_W)[:, None] + jnp.arange(NA_WIN_W)
    bias_c = rpb[:, :, key_cols - cols[:, None] + NA_WIN_W - 1]

    def row_block(r):
        key_rows = row0[r] + jnp.arange(wh)
        kb = jnp.take(kg, key_rows, axis=1)[:, :, key_cols]
        vb = jnp.take(vg, key_rows, axis=1)[:, :, key_cols]
        qb = lax.dynamic_index_in_dim(qg, r, axis=1, keepdims=False)
        bias = jnp.take(bias_c, key_rows - r + NA_WIN_H - 1, axis=1)
        s_loc = jnp.einsum('bchd,bicjhd->bhcij', qb, kb) * scale + jnp.transpose(bias, (0, 2, 1, 3))
        s_ctx = jnp.einsum('bchd,blhd->bhcl', qb, kc) * scale
        s = jnp.concatenate([s_loc.reshape(B, H, GRID_W, n_loc).astype(jnp.float32),
                             s_ctx.astype(jnp.float32)], axis=-1)
        p = jax.nn.softmax(s, axis=-1).astype(v.dtype)
        p_loc = p[..., :n_loc].reshape(B, H, GRID_W, wh, NA_WIN_W)
        return (jnp.einsum('bhcij,bicjhd->bchd', p_loc, vb)
                + jnp.einsum('bhcl,blhd->bchd', p[..., n_loc:], vc))

    out = lax.map(row_block, jnp.arange(rows))
    return jnp.transpose(out, (1, 0, 2, 3, 4)).reshape(B, S, H * Dh)


def context_attention(qc, kc, vc):
    s = jnp.einsum('blhd,bmhd->bhlm', qc, kc) * qc.shape[-1] ** -0.5
    p = jax.nn.softmax(s.astype(jnp.float32), axis=-1).astype(vc.dtype)
    return jnp.einsum('bhlm,bmhd->blhd', p, vc).reshape(qc.shape[0], qc.shape[1], -1)


def centred_shift(p, mu_prev, mu_next):
    zero = jnp.zeros_like(p[:, :1])
    prev = jnp.concatenate([zero, p[:, :-1]], axis=1)
    nxt = jnp.concatenate([p[:, 1:], zero], axis=1)
    return p + mu_prev * (prev - p) + mu_next * (nxt - p)


def _dir_time_major(t):
    B, T, _, _ = t.shape
    t = jnp.stack([t[:, :, 0], jnp.flip(t[:, :, 1], axis=1)], axis=0)
    return jnp.transpose(t.reshape(2, B, T, RWKV_HEADS, RWKV_HEAD_DIM), (2, 0, 1, 3, 4))


def rwkv_prep(p, mu_prev, mu_next, w0, w2, a0, a2, g2, k_k, k_a):
    f32 = jnp.float32
    p = centred_shift(p, mu_prev, mu_next)
    r, k, v, wc, ac, gc = jnp.split(p, RWKV_SPLITS, axis=-1)
    B, T, C = r.shape
    wc = wc.reshape(B, T, 2, DECAY_LORA)
    ac = ac.reshape(B, T, 2, AAA_LORA)
    w_log = -jax.nn.softplus(-(w0 + jnp.einsum('btdr,drc->btdc', jnp.tanh(wc), w2)).astype(f32)) - 0.5
    decay = jnp.exp(-jnp.exp(w_log))
    a = jax.nn.sigmoid((a0 + jnp.einsum('btdr,drc->btdc', ac, a2)).astype(f32))
    g = jax.nn.sigmoid(gc) @ g2
    kk = (k * k_k).astype(f32).reshape(B, T, RWKV_HEADS, RWKV_HEAD_DIM)
    kk = (kk * lax.rsqrt(jnp.sum(kk * kk, axis=-1, keepdims=True) + 1e-12)).reshape(B, T, C)
    kd = k.astype(f32)[:, :, None] * (1.0 + (a - 1.0) * k_a)
    shape = kd.shape
    rd = jnp.broadcast_to(r.astype(f32)[:, :, None], shape)
    vd = jnp.broadcast_to(v.astype(f32)[:, :, None], shape)
    kkd = jnp.broadcast_to(kk[:, :, None], shape)
    scan_in = (_dir_time_major(rd), _dir_time_major(decay), _dir_time_major(kd),
               _dir_time_major(vd), _dir_time_major(kkd), _dir_time_major(a))
    return scan_in, (r, kd, v, g)


def delta_scan(state0, inputs, emit):
    def step(S, inp):
        r_t, w_t, k_t, v_t, kk_t, a_t = inp
        s_kk = jnp.einsum('...vk,...k->...v', S, kk_t)
        S = (S * w_t[..., None, :] - s_kk[..., :, None] * (kk_t * a_t)[..., None, :]
             + v_t[..., :, None] * k_t[..., None, :])
        return S, (jnp.einsum('...vk,...k->...v', S, r_t) if emit else None)
    return lax.scan(step, state0, inputs)


def rwkv_readout(ys, r, kd, v, g, r_k, gn_g, gn_b):
    B, T, C = r.shape
    H, N = RWKV_HEADS, RWKV_HEAD_DIM
    y = jnp.transpose(ys[:, 0] + jnp.flip(ys[:, 1], axis=0), (1, 0, 2, 3))
    mu = jnp.mean(y, axis=-1, keepdims=True)
    var = jnp.mean(jnp.square(y - mu), axis=-1, keepdims=True)
    y = (y - mu) * lax.rsqrt(var + RWKV_GN_EPS) * gn_g.reshape(H, N) + gn_b.reshape(H, N)
    bonus = jnp.sum(r.astype(jnp.float32).reshape(B, T, 1, H, N) * kd.reshape(B, T, 2, H, N) * r_k,
                    axis=(2, 4))
    y = y + bonus[..., None] * v.reshape(B, T, H, N)
    return (y.reshape(B, T, C) * g).astype(r.dtype)


def spatial_gating(p, ln_g, ln_b, w_s, b_s):
    u, v = jnp.split(jax.nn.gelu(p), 2, axis=-1)
    v = layer_norm(v, ln_g, ln_b)
    B, T, C = v.shape
    vc = v.reshape(B, T // SGU_CHUNK, SGU_CHUNK, SGU_GROUPS, C // SGU_GROUPS)
    vm = jnp.einsum('gpq,bnqgc->bnpgc', w_s, vc) + jnp.transpose(b_s)[:, :, None]
    return u * vm.reshape(B, T, C)


def merge_branches(gates, o_na, o_rwkv, o_sgu, w_branch, w_out):
    g_na, g_rwkv, g_sgu = jnp.split(jax.nn.sigmoid(gates), N_BRANCH, axis=-1)
    y = g_na * (o_na @ w_branch[0]) + g_rwkv * (o_rwkv @ w_branch[1]) + g_sgu * (o_sgu @ w_branch[2])
    return y @ w_out


def token_mixer(h, hc, w_in, rpb, mu_prev, mu_next, w0, w2, a0, a2, g2, k_k, k_a, r_k, gn_g, gn_b,
                sgu_ln_g, sgu_ln_b, sgu_w, sgu_b, w_branch, w_out, ctx_out):
    B = h.shape[0]
    gates, q, k, v, p_rwkv, p_sgu = jnp.split(h @ w_in, IN_SPLITS, axis=-1)
    gates_c, q_c, k_c, v_c, p_rwkv_c, p_sgu_c = jnp.split(hc @ w_in, IN_SPLITS, axis=-1)
    kc_h, vc_h = _heads(k_c), _heads(v_c)
    o_na = neighbourhood_attention(_heads(q), _heads(k), _heads(v), kc_h, vc_h, rpb)
    rw = (mu_prev, mu_next, w0, w2, a0, a2, g2, k_k, k_a)
    scan_c, read_c = rwkv_prep(p_rwkv_c, *rw)
    scan_l, read_l = rwkv_prep(p_rwkv, *rw)
    state0 = jnp.zeros((2, B, RWKV_HEADS, RWKV_HEAD_DIM, RWKV_HEAD_DIM), jnp.float32)
    state_ctx, ys_c = delta_scan(state0, scan_c, ctx_out)
    _, ys_l = delta_scan(state_ctx, scan_l, True)
    o_rwkv = rwkv_readout(ys_l, *read_l, r_k, gn_g, gn_b)
    o_sgu = spatial_gating(p_sgu, sgu_ln_g, sgu_ln_b, sgu_w, sgu_b)
    y = merge_branches(gates, o_na, o_rwkv, o_sgu, w_branch, w_out)
    if not ctx_out:
        return y, None
    o_na_c = context_attention(_heads(q_c), kc_h, vc_h)
    o_rwkv_c = rwkv_readout(ys_c, *read_c, r_k, gn_g, gn_b)
    o_sgu_c = spatial_gating(p_sgu_c, sgu_ln_g, sgu_ln_b, sgu_w, sgu_b)
    y_c = merge_branches(gates_c, o_na_c, o_rwkv_c, o_sgu_c, w_branch, w_out)
    return y, y_c


def swiglu(h, w_gu, w_down):
    gate, up = jnp.split(h @ w_gu, 2, axis=-1)
    return (jax.nn.silu(gate) * up) @ w_down


def setup_inputs(seed: int = 0) -> dict:
    key = jax.random.key(seed)
    ks = iter(jax.random.split(key, 32))
    L, D, W = DEPTH, D_MODEL, BRANCH_WIDTH

    def nrm(shape, s):
        return s * jax.random.normal(next(ks), shape, jnp.float32)

    def uni(shape, lo, hi):
        return jax.random.uniform(next(ks), shape, jnp.float32, minval=lo, maxval=hi)

    return {
        'x': nrm((BATCH, SEQ, D), 1.0),
        'c': nrm((BATCH, D), 1.0),
        'ctx': nrm((BATCH, CTX_LEN, D), 1.0),
        'c_ctx': nrm((D,), 1.0),
        'w_ada': nrm((L, D, 6 * D), 0.5 * D ** -0.5),
        'b_ada': nrm((L, 6 * D), 0.02),
        'w_in': nrm((L, D, D_IN), D ** -0.5),
        'na_rpb': nrm((L, NA_HEADS, 2 * NA_WIN_H - 1, 2 * NA_WIN_W - 1), 0.5),
        'rwkv_mu_prev': uni((L, RWKV_IN), 0.0, 0.5),
        'rwkv_mu_next': uni((L, RWKV_IN), 0.0, 0.5),
        'rwkv_w0': uni((L, 2, RWKV_WIDTH), -6.0, 1.0),
        'rwkv_w2': nrm((L, 2, DECAY_LORA, RWKV_WIDTH), 0.5 * DECAY_LORA ** -0.5),
        'rwkv_a0': nrm((L, 2, RWKV_WIDTH), 0.1),
        'rwkv_a2': nrm((L, 2, AAA_LORA, RWKV_WIDTH), 0.5 * AAA_LORA ** -0.5),
        'rwkv_g2': nrm((L, GATE_LORA, RWKV_WIDTH), GATE_LORA ** -0.5),
        'rwkv_k_k': 0.85 + nrm((L, RWKV_WIDTH), 0.05),
        'rwkv_k_a': 1.0 + nrm((L, RWKV_WIDTH), 0.05),
        'rwkv_r_k': nrm((L, RWKV_HEADS, RWKV_HEAD_DIM), 0.1),
        'rwkv_gn_g': 1.0 + nrm((L, RWKV_WIDTH), 0.05),
        'rwkv_gn_b': nrm((L, RWKV_WIDTH), 0.02),
        'sgu_ln_g': 1.0 + nrm((L, SGU_WIDTH), 0.05),
        'sgu_ln_b': nrm((L, SGU_WIDTH), 0.02),
        'sgu_w': nrm((L, SGU_GROUPS, SGU_CHUNK, SGU_CHUNK), SGU_CHUNK ** -0.5),
        'sgu_b': 1.0 + nrm((L, SGU_GROUPS, SGU_CHUNK), 0.1),
        'w_branch': nrm((L, N_BRANCH, W, D), BETA * W ** -0.5),
        'w_out': nrm((L, D, D), BETA * D ** -0.5),
        'ln1_g': 1.0 + nrm((L, D), 0.05),
        'ln1_b': nrm((L, D), 0.02),
        'ln2_g': 1.0 + nrm((L, D), 0.05),
        'ln2_b': nrm((L, D), 0.02),
        'ffn_w_gu': nrm((L, D, 2 * D_FF), D ** -0.5),
        'ffn_w_down': nrm((L, D_FF, D), BETA * D_FF ** -0.5),
    }


def reference(x, c, ctx, c_ctx, w_ada, b_ada, w_in, na_rpb, rwkv_mu_prev, rwkv_mu_next, rwkv_w0, rwkv_w2,
              rwkv_a0, rwkv_a2, rwkv_g2, rwkv_k_k, rwkv_k_a, rwkv_r_k, rwkv_gn_g, rwkv_gn_b, sgu_ln_g, sgu_ln_b,
              sgu_w, sgu_b, w_branch, w_out, ln1_g, ln1_b, ln2_g, ln2_b, ffn_w_gu, ffn_w_down):
    silu_c = jax.nn.silu(c)
    silu_cc = jax.nn.silu(c_ctx)
    xc = ctx
    for l in range(DEPTH):
        last = l == DEPTH - 1
        sh_m, sc_m, g_m, sh_f, sc_f, g_f = jnp.split((silu_c @ w_ada[l] + b_ada[l])[:, None, :], 6, axis=-1)
        csh_m, csc_m, cg_m, csh_f, csc_f, cg_f = jnp.split(silu_cc @ w_ada[l] + b_ada[l], 6, axis=-1)
        y, y_c = token_mixer(x * (1.0 + sc_m) + sh_m, xc * (1.0 + csc_m) + csh_m, w_in[l], na_rpb[l],
                             rwkv_mu_prev[l], rwkv_mu_next[l], rwkv_w0[l], rwkv_w2[l], rwkv_a0[l], rwkv_a2[l],
                             rwkv_g2[l], rwkv_k_k[l], rwkv_k_a[l], rwkv_r_k[l], rwkv_gn_g[l], rwkv_gn_b[l],
                             sgu_ln_g[l], sgu_ln_b[l], sgu_w[l], sgu_b[l], w_branch[l], w_out[l], not last)
        x = layer_norm(ALPHA * x + g_m * y, ln1_g[l], ln1_b[l])
        x = layer_norm(ALPHA * x + g_f * swiglu(x * (1.0 + sc_f) + sh_f, ffn_w_gu[l], ffn_w_down[l]),
                       ln2_g[l], ln2_b[l])
        if not last:
            xc = layer_norm(ALPHA * xc + cg_m * y_c, ln1_g[l], ln1_b[l])
            xc = layer_norm(ALPHA * xc + cg_f * swiglu(xc * (1.0 + csc_f) + csh_f, ffn_w_gu[l], ffn_w_down[l]),
                            ln2_g[l], ln2_b[l])
    return x
```

```python
import functools

import jax
import jax.numpy as jnp
import numpy as np
from jax import lax
from jax.experimental import pallas as pl
from jax.experimental.pallas import tpu as pltpu

F32 = jnp.float32
BF16 = jnp.bfloat16

D_MODEL = 1024
GRID_W = 64
NA_HEADS = 8
HEAD_DIM = 64
NA_WIN_H = 8
NA_WIN_W = 16
BRANCH_WIDTH = 512
DECAY_LORA = 64
GATE_LORA = 128
RWKV_IN = 3 * BRANCH_WIDTH + 4 * DECAY_LORA + GATE_LORA
RWKV_GN_EPS = 64e-5
SGU_CHUNK = 128
SGU_GROUPS = 8
D_FF = 2816
LN_EPS = 1e-5
MASK_VALUE = -1e30

LANES = 128
TOKEN_TILE = 256
RWKV_CHUNK = 64
N_PAIRS = NA_HEADS // 2
HIGHEST = lax.Precision.HIGHEST
VMEM_LIMIT = 56 * 1024 * 1024

_NT = (((1,), (1,)), ((), ()))
_TN = (((0,), (0,)), ((), ()))


def _params(sem, vmem=VMEM_LIMIT):
    return pltpu.CompilerParams(dimension_semantics=sem, vmem_limit_bytes=vmem)


def _dot(a, b):
    return jnp.dot(a.astype(BF16), b.astype(BF16), preferred_element_type=F32)


def _dot_hi(a, b):
    return jnp.dot(a, b, preferred_element_type=F32, precision=HIGHEST)


def _layer_norm(x, g, b, eps):
    mu = jnp.mean(x, axis=-1, keepdims=True)
    xc = x - mu
    var = jnp.mean(xc * xc, axis=-1, keepdims=True)
    return xc * lax.rsqrt(var + eps) * g + b


def _sigmoid(x):
    return 1.0 / (1.0 + jnp.exp(-x))


def _head_sum(x):
    rows = x.shape[0]
    lo = lax.broadcasted_iota(jnp.int32, (rows, LANES), 1) < HEAD_DIM
    out = []
    for p in range(x.shape[1] // LANES):
        xp = x[:, p * LANES:(p + 1) * LANES]
        s_lo = jnp.sum(jnp.where(lo, xp, 0.0), axis=-1, keepdims=True)
        s_hi = jnp.sum(jnp.where(lo, 0.0, xp), axis=-1, keepdims=True)
        out.append(jnp.where(lo, s_lo, s_hi))
    return jnp.concatenate(out, axis=1)


def _ada_kernel(c_ref, w_ref, b_ref, o_ref):
    c = c_ref[...]
    o_ref[...] = _dot(c * _sigmoid(c), w_ref[...]) + b_ref[...]


def _ada_all(cond, w_ada, b_ada):
    depth, d, n = w_ada.shape
    tn = n // 4
    return pl.pallas_call(
        _ada_kernel,
        grid=(depth, n // tn),
        in_specs=[
            pl.BlockSpec((8, d), lambda l, i: (0, 0)),
            pl.BlockSpec((None, d, tn), lambda l, i: (l, 0, i)),
            pl.BlockSpec((None, 1, tn), lambda l, i: (l, 0, i)),
        ],
        out_specs=pl.BlockSpec((None, 8, tn), lambda l, i: (l, 0, i)),
        out_shape=jax.ShapeDtypeStruct((depth, 8, n), F32),
        compiler_params=_params(("parallel", "parallel")),
        name="adaln",
    )(cond, w_ada, b_ada.reshape(depth, 1, n))


_IN_SPLITS = (0, 3 * D_MODEL, 3 * D_MODEL + 512, 3 * D_MODEL + 1024, 3 * D_MODEL + 1536,
              3 * D_MODEL + 1536 + RWKV_IN, 3 * D_MODEL + 1536 + RWKV_IN + 2 * BRANCH_WIDTH)
_IN_COL_CHUNK = 512


def _inproj_kernel(x_ref, mod_ref, w_ref, gates_ref, q_ref, k_ref, v_ref, pr_ref, ps_ref):
    h = (x_ref[...] * (1.0 + mod_ref[1:2, :]) + mod_ref[0:1, :]).astype(BF16)
    outs = (gates_ref, q_ref, k_ref, v_ref, pr_ref, ps_ref)
    for idx, o_ref in enumerate(outs):
        lo, hi = _IN_SPLITS[idx], _IN_SPLITS[idx + 1]
        scale = HEAD_DIM ** -0.5 if idx == 1 else None
        for c0 in range(lo, hi, _IN_COL_CHUNK):
            c1 = min(c0 + _IN_COL_CHUNK, hi)
            r = jnp.dot(h, w_ref[:, c0:c1], preferred_element_type=F32)
            if scale is not None:
                r = r * scale
            o_ref[:, c0 - lo:c1 - lo] = r.astype(o_ref.dtype)


def _mod_row(n_ctx_tiles, batch):
    return lambda b, j: jnp.where(j < n_ctx_tiles, batch, b)


def _inproj(x, mods, w_in, layer, n_ctx_tiles):
    batch, t, d = x.shape
    nt = t // TOKEN_TILE
    d_in = w_in.shape[-1]
    row = _mod_row(n_ctx_tiles, batch)
    widths = [_IN_SPLITS[i + 1] - _IN_SPLITS[i] for i in range(6)]
    dtypes = [F32, BF16, BF16, BF16, F32, F32]
    return pl.pallas_call(
        _inproj_kernel,
        grid=(batch, nt),
        in_specs=[
            pl.BlockSpec((None, TOKEN_TILE, d), lambda b, j: (b, j, 0)),
            pl.BlockSpec((None, None, 6, d), lambda b, j: (layer, row(b, j), 0, 0)),
            pl.BlockSpec((None, d, d_in), lambda b, j: (layer, 0, 0), pipeline_mode=pl.Buffered(1)),
        ],
        out_specs=[pl.BlockSpec((None, TOKEN_TILE, w), lambda b, j: (b, j, 0)) for w in widths],
        out_shape=[jax.ShapeDtypeStruct((batch, t, w), dt) for w, dt in zip(widths, dtypes)],
        compiler_params=_params(("parallel", "parallel")),
        name="inproj",
    )(x, mods, w_in)


def _na_bias_table(rpb):
    cols = np.arange(GRID_W)
    col0 = np.clip(cols - NA_WIN_W // 2, 0, GRID_W - NA_WIN_W)
    kc = np.arange(GRID_W)
    in_win = (kc[None, :] >= col0[:, None]) & (kc[None, :] < col0[:, None] + NA_WIN_W)
    dcol = np.clip(kc[None, :] - cols[:, None] + NA_WIN_W - 1, 0, 2 * NA_WIN_W - 2)
    e = np.arange(NA_WIN_H)
    drow = np.arange(NA_WIN_H)[None, :] - e[:, None] + NA_WIN_H - 1
    tab = rpb[:, drow[:, :, None, None], dcol[None, None, :, :]]
    tab = jnp.where(in_win[None, None, None], tab, MASK_VALUE)
    tab = jnp.transpose(tab, (1, 0, 3, 2, 4))
    return tab.reshape(NA_WIN_H, NA_HEADS, GRID_W, NA_WIN_H * GRID_W)


def _na_kernel(q_ref, k_ref, v_ref, bias_ref, o_ref, *, n_ctx, rows):
    j = pl.program_id(1)
    n_ctx_q = n_ctx // GRID_W
    lo = lax.broadcasted_iota(jnp.int32, (GRID_W, LANES), 1) < HEAD_DIM
    n_win = NA_WIN_H * GRID_W

    def attend(local):
        if local:
            r = j - n_ctx_q
            row0 = jnp.clip(r - NA_WIN_H // 2, 0, rows - NA_WIN_H)
            start = pl.multiple_of(n_ctx + row0 * GRID_W, GRID_W)
        for p in range(N_PAIRS):
            cs = slice(p * LANES, (p + 1) * LANES)
            qp = q_ref[:, cs].astype(F32)
            kc = k_ref[0:n_ctx, cs]
            vc = v_ref[0:n_ctx, cs]
            if local:
                kw = k_ref[pl.ds(start, n_win), cs]
                vw = v_ref[pl.ds(start, n_win), cs]
            outs = []
            for hh in range(2):
                qm = jnp.where(lo if hh == 0 else ~lo, qp, 0.0).astype(BF16)
                s_c = lax.dot_general(qm, kc, _NT, preferred_element_type=F32)
                mx = jnp.max(s_c, axis=-1, keepdims=True)
                if local:
                    s_l = lax.dot_general(qm, kw, _NT, preferred_element_type=F32) + bias_ref[2 * p + hh]
                    mx = jnp.maximum(mx, jnp.max(s_l, axis=-1, keepdims=True))
                    p_l = jnp.exp(s_l - mx)
                p_c = jnp.exp(s_c - mx)
                den = jnp.sum(p_c, axis=-1, keepdims=True)
                acc = jnp.dot(p_c.astype(BF16), vc, preferred_element_type=F32)
                if local:
                    den = den + jnp.sum(p_l, axis=-1, keepdims=True)
                    acc = acc + jnp.dot(p_l.astype(BF16), vw, preferred_element_type=F32)
                outs.append(acc / den)
            o_ref[:, cs] = jnp.where(lo, outs[0], outs[1])

    @pl.when(j < n_ctx_q)
    def _():
        attend(False)

    @pl.when(j >= n_ctx_q)
    def _():
        attend(True)


def _na_attention(q, k, v, bias, n_ctx):
    batch, t, w = q.shape
    rows = (t - n_ctx) // GRID_W
    n_ctx_q = n_ctx // GRID_W

    def bias_idx(b, j):
        r = jnp.maximum(j - n_ctx_q, 0)
        row0 = jnp.clip(r - NA_WIN_H // 2, 0, rows - NA_WIN_H)
        return (r - row0, 0, 0, 0)

    return pl.pallas_call(
        functools.partial(_na_kernel, n_ctx=n_ctx, rows=rows),
        grid=(batch, t // GRID_W),
        in_specs=[
            pl.BlockSpec((None, GRID_W, w), lambda b, j: (b, j, 0)),
            pl.BlockSpec((None, t, w), lambda b, j: (b, 0, 0)),
            pl.BlockSpec((None, t, w), lambda b, j: (b, 0, 0)),
            pl.BlockSpec((None, NA_HEADS, GRID_W, NA_WIN_H * GRID_W), bias_idx),
        ],
        out_specs=pl.BlockSpec((None, GRID_W, w), lambda b, j: (b, j, 0)),
        out_shape=jax.ShapeDtypeStruct((batch, t, w), F32),
        compiler_params=_params(("parallel", "arbitrary")),
        name="na_attention",
    )(q, k, v, bias)


def _gelu_tanh(x):
    return x * (0.5 * (1.0 + jnp.tanh(0.7978845608028654 * (x + 0.044715 * (x * x * x)))))


def _sgu_kernel(p_ref, lng_ref, lnb_ref, w_ref, b_ref, o_ref):
    g = _gelu_tanh(p_ref[...])
    u = g[:, :BRANCH_WIDTH]
    v = _layer_norm(g[:, BRANCH_WIDTH:], lng_ref[...], lnb_ref[...], LN_EPS).astype(BF16)
    lo = lax.broadcasted_iota(jnp.int32, (SGU_CHUNK, LANES), 1) < HEAD_DIM
    for p in range(N_PAIRS):
        cs = slice(p * LANES, (p + 1) * LANES)
        vp = v[:, cs]
        r0 = jnp.dot(w_ref[2 * p], vp, preferred_element_type=F32)
        r1 = jnp.dot(w_ref[2 * p + 1], vp, preferred_element_type=F32)
        o_ref[:, cs] = u[:, cs] * (jnp.where(lo, r0, r1) + b_ref[:, cs])


def _sgu(p_sgu, ln_g, ln_b, w_s, b_full):
    batch, t, w = p_sgu.shape
    const2 = lambda b, j: (0, 0)
    return pl.pallas_call(
        _sgu_kernel,
        grid=(batch, t // SGU_CHUNK),
        in_specs=[
            pl.BlockSpec((None, SGU_CHUNK, w), lambda b, j: (b, j, 0)),
            pl.BlockSpec((1, BRANCH_WIDTH), const2),
            pl.BlockSpec((1, BRANCH_WIDTH), const2),
            pl.BlockSpec((SGU_GROUPS, SGU_CHUNK, SGU_CHUNK), lambda b, j: (0, 0, 0)),
            pl.BlockSpec((SGU_CHUNK, BRANCH_WIDTH), const2),
        ],
        out_specs=pl.BlockSpec((None, SGU_CHUNK, BRANCH_WIDTH), lambda b, j: (b, j, 0)),
        out_shape=jax.ShapeDtypeStruct((batch, t, BRANCH_WIDTH), F32),
        compiler_params=_params(("parallel", "parallel")),
        name="sgu",
    )(p_sgu, ln_g, ln_b, w_s, b_full)


_N_STAGE = 8


def _rwkv_prep_kernel(p_ref, prev_ref, next_ref, mup_ref, mun_ref, w0_ref, w2_ref, a0_ref, a2_ref, g2_ref,
                      kk_ref, ka_ref, rk_ref,
                      q1_ref, y0_ref, m_ref, n0_ref, g_ref, bv_ref, stage_ref, *, n_ctx_tiles, n_tiles):
    j = pl.program_id(1)
    tm = TOKEN_TILE
    w = BRANCH_WIDTH
    ch = RWKV_CHUNK

    p = p_ref[...]
    first = (j == 0) | (j == n_ctx_tiles)
    last = (j == n_ctx_tiles - 1) | (j == n_tiles - 1)
    prev_row = jnp.where(first, 0.0, prev_ref[7:8, :])
    next_row = jnp.where(last, 0.0, next_ref[0:1, :])
    row = lax.broadcasted_iota(jnp.int32, (tm, 1), 0)
    p_prev = jnp.where(row == 0, prev_row, pltpu.roll(p, 1, 0))
    p_next = jnp.where(row == tm - 1, next_row, pltpu.roll(p, tm - 1, 0))
    ps = p + mup_ref[...] * (p_prev - p) + mun_ref[...] * (p_next - p)

    r = ps[:, 0:w]
    k = ps[:, w:2 * w]
    v = ps[:, 2 * w:3 * w]
    wc = jnp.tanh(ps[:, 3 * w:3 * w + LANES])
    ac = ps[:, 3 * w + LANES:3 * w + 2 * LANES]
    gc = ps[:, 3 * w + 2 * LANES:3 * w + 3 * LANES]

    g_ref[...] = _dot(_sigmoid(gc), g2_ref[...])
    kk = k * kk_ref[...]
    kk = kk * lax.rsqrt(_head_sum(kk * kk) + 1e-12)

    lo_dir = lax.broadcasted_iota(jnp.int32, (tm, LANES), 1) < DECAY_LORA
    ti = lax.broadcasted_iota(jnp.int32, (tm, tm), 0)
    si = lax.broadcasted_iota(jnp.int32, (tm, tm), 1)
    same_chunk = (ti // ch) == (si // ch)
    ones_blk = jnp.where(same_chunk, 1.0, 0.0)
    ci = lax.broadcasted_iota(jnp.int32, (ch, ch), 0)
    cj = lax.broadcasted_iota(jnp.int32, (ch, ch), 1)
    eye_c = jnp.where(ci == cj, 1.0, 0.0)
    lo = lax.broadcasted_iota(jnp.int32, (ch, LANES), 1) < HEAD_DIM
    di = lax.broadcasted_iota(jnp.int32, (LANES, LANES), 0)
    dj = lax.broadcasted_iota(jnp.int32, (LANES, LANES), 1)
    diag128 = di == dj

    kd_sum = jnp.zeros((tm, w), F32)
    for d in range(2):
        md = lo_dir if d == 0 else ~lo_dir
        z = w0_ref[d:d + 1, :] + _dot(jnp.where(md, wc, 0.0), w2_ref[...])
        w_log = -(jnp.maximum(-z, 0.0) + jnp.log(1.0 + jnp.exp(-jnp.abs(z)))) - 0.5
        lw = -jnp.exp(w_log)
        a = _sigmoid(a0_ref[d:d + 1, :] + _dot(jnp.where(md, ac, 0.0), a2_ref[...]))
        kd = k * (1.0 + (a - 1.0) * ka_ref[...])
        kd_sum = kd_sum + kd
        beta = kk * a
        before = (si <= ti) if d == 0 else (si >= ti)
        tri = jnp.where(same_chunk & before, 1.0, 0.0)
        cum = _dot_hi(tri, lw)
        tot = _dot_hi(ones_blk, lw)
        e_neg = jnp.exp(-cum)
        e_hat = jnp.exp(tot - cum)
        stage_ref[0] = -kk * jnp.exp(cum - lw)
        stage_ref[1] = r * jnp.exp(cum)
        stage_ref[2] = beta * e_neg
        stage_ref[3] = kd * e_neg
        stage_ref[4] = beta * e_hat
        stage_ref[5] = kd * e_hat
        stage_ref[6] = v
        stage_ref[7] = jnp.exp(tot)
        strict = (cj < ci) if d == 0 else (cj > ci)
        incl = (cj <= ci) if d == 0 else (cj >= ci)

        def chunk_body(c, carry):
            rs = pl.ds(pl.multiple_of(c * ch, ch), ch)
            for pr in range(N_PAIRS):
                cs = slice(pr * LANES, (pr + 1) * LANES)
                ab, rb, bt, kt, bh, kh, vv, dg = [stage_ref[i, rs, cs] for i in range(_N_STAGE)]
                bt_b = bt.astype(BF16)
                kt_b = kt.astype(BF16)
                q1_p = jnp.zeros((ch, LANES), F32)
                y0_p = jnp.zeros((ch, LANES), F32)
                m_p = jnp.where(diag128, jnp.broadcast_to(dg[0:1, :], (LANES, LANES)), 0.0)
                n_p = jnp.zeros((LANES, LANES), F32)
                for hh in range(2):
                    m = lo if hh == 0 else ~lo
                    abm = jnp.where(m, ab, 0.0)
                    rbm = jnp.where(m, rb, 0.0)
                    vm = jnp.where(m, vv, 0.0)
                    ar = jnp.concatenate([abm, rbm], axis=0).astype(BF16)
                    g1 = lax.dot_general(ar, bt_b, _NT, preferred_element_type=F32)
                    g2 = lax.dot_general(ar, kt_b, _NT, preferred_element_type=F32)
                    a_ab = jnp.where(strict, g1[:ch], 0.0)
                    a_rb = jnp.where(incl, g1[ch:], 0.0)
                    a_ak = jnp.where(strict, g2[:ch], 0.0)
                    a_rk = jnp.where(incl, g2[ch:], 0.0)
                    tinv = eye_c + a_ab
                    apow = a_ab
                    for _ in range(5):
                        apow = _dot(apow, apow)
                        tinv = tinv + _dot(tinv, apow)
                    zv = _dot(jnp.concatenate([a_ak, a_rk], axis=0), vm)
                    x = jnp.concatenate([abm, zv[:ch]], axis=1)
                    wmat = _dot(tinv, x).astype(BF16)
                    qy = jnp.dot(a_rb.astype(BF16), wmat, preferred_element_type=F32)
                    q1_p = q1_p + rbm + qy[:, :LANES]
                    y0_p = y0_p + qy[:, LANES:] + zv[ch:]
                    mn = lax.dot_general(jnp.where(m, bh, 0.0).astype(BF16), wmat, _TN,
                                         preferred_element_type=F32)
                    nb = lax.dot_general(jnp.where(m, kh, 0.0).astype(BF16), vm.astype(BF16), _TN,
                                         preferred_element_type=F32)
                    m_p = m_p + mn[:, :LANES]
                    n_p = n_p + mn[:, LANES:] + nb
                q1_ref[d, rs, cs] = q1_p
                y0_ref[d, rs, cs] = y0_p
                m_ref[d, c, pr] = m_p
                n0_ref[d, c, pr] = n_p
            return carry

        lax.fori_loop(0, tm // ch, chunk_body, 0)

    bv_ref[...] = _head_sum(r * kd_sum * rk_ref[...]) * v


def _rwkv_prep(p_rwkv, prm, n_ctx_tiles):
    batch, t, wp = p_rwkv.shape
    nt = t // TOKEN_TILE
    cpt = TOKEN_TILE // RWKV_CHUNK
    nc = t // RWKV_CHUNK
    halo = TOKEN_TILE // 8
    w = BRANCH_WIDTH
    c2 = lambda b, j: (0, 0)
    vec = lambda n: pl.BlockSpec((1, n), c2)
    tok = pl.BlockSpec((None, TOKEN_TILE, w), lambda b, j: (b, j, 0))
    dtok = pl.BlockSpec((None, 2, TOKEN_TILE, w), lambda b, j: (b, 0, j, 0))
    mat = pl.BlockSpec((None, 2, cpt, N_PAIRS, LANES, LANES), lambda b, j: (b, 0, j, 0, 0, 0))
    return pl.pallas_call(
        functools.partial(_rwkv_prep_kernel, n_ctx_tiles=n_ctx_tiles, n_tiles=nt),
        grid=(batch, nt),
        in_specs=[
            pl.BlockSpec((None, TOKEN_TILE, wp), lambda b, j: (b, j, 0)),
            pl.BlockSpec((None, 8, wp), lambda b, j: (b, jnp.maximum(j * halo - 1, 0), 0)),
            pl.BlockSpec((None, 8, wp), lambda b, j: (b, jnp.minimum((j + 1) * halo, t // 8 - 1), 0)),
            vec(wp), vec(wp),
            pl.BlockSpec((2, w), c2), pl.BlockSpec((2 * DECAY_LORA, w), c2),
            pl.BlockSpec((2, w), c2), pl.BlockSpec((2 * DECAY_LORA, w), c2),
            pl.BlockSpec((GATE_LORA, w), c2),
            vec(w), vec(w), vec(w),
        ],
        out_specs=[dtok, dtok, mat, mat, tok, tok],
        out_shape=[
            jax.ShapeDtypeStruct((batch, 2, t, w), F32),
            jax.ShapeDtypeStruct((batch, 2, t, w), F32),
            jax.ShapeDtypeStruct((batch, 2, nc, N_PAIRS, LANES, LANES), F32),
            jax.ShapeDtypeStruct((batch, 2, nc, N_PAIRS, LANES, LANES), F32),
            jax.ShapeDtypeStruct((batch, t, w), F32),
            jax.ShapeDtypeStruct((batch, t, w), F32),
        ],
        scratch_shapes=[pltpu.VMEM((_N_STAGE, TOKEN_TILE, w), F32)],
        compiler_params=_params(("parallel", "parallel")),
        name="rwkv_prep",
    )(p_rwkv, p_rwkv, p_rwkv, prm["mu_prev"], prm["mu_next"], prm["w0"], prm["w2"], prm["a0"], prm["a2"],
      prm["g2"], prm["k_k"], prm["k_a"], prm["r_k"])


def _rwkv_chain_kernel(q1_ref, y0_ref, m_ref, n0_ref, y_ref, s_ref):
    @pl.when(pl.program_id(2) == 0)
    def _():
        s_ref[...] = jnp.zeros_like(s_ref)

    for pr in range(N_PAIRS):
        cs = slice(pr * LANES, (pr + 1) * LANES)
        s = s_ref[pr]
        y_ref[:, cs] = _dot_hi(q1_ref[:, cs], s) + y0_ref[:, cs]
        s_ref[pr] = _dot_hi(m_ref[pr], s) + n0_ref[pr]


def _rwkv_chain(q1, y0, m, n0, n_ctx):
    batch, _, t, w = q1.shape
    nc = t // RWKV_CHUNK
    ncc = n_ctx // RWKV_CHUNK

    def chunk(d, s):
        back = jnp.where(s < ncc, ncc - 1 - s, nc - 1 - (s - ncc))
        return jnp.where(d == 0, s, back)

    tok = pl.BlockSpec((None, None, RWKV_CHUNK, w), lambda b, d, s: (b, d, chunk(d, s), 0))
    mat = pl.BlockSpec((None, None, None, N_PAIRS, LANES, LANES), lambda b, d, s: (b, d, chunk(d, s), 0, 0, 0))
    return pl.pallas_call(
        _rwkv_chain_kernel,
        grid=(batch, 2, nc),
        in_specs=[tok, tok, mat, mat],
        out_specs=tok,
        out_shape=jax.ShapeDtypeStruct((batch, 2, t, w), F32),
        scratch_shapes=[pltpu.VMEM((N_PAIRS, LANES, LANES), F32)],
        compiler_params=_params(("parallel", "parallel", "arbitrary")),
        name="rwkv_chain",
    )(q1, y0, m, n0)


def _merge_kernel(x_ref, mod_ref, gates_ref, ona_ref, y_ref, g_ref, bv_ref, osgu_ref, gng_ref, gnb_ref,
                  wb_ref, wo_ref, lng_ref, lnb_ref, o_ref, *, alpha):
    y = y_ref[0] + y_ref[1]
    mu = _head_sum(y) * (1.0 / HEAD_DIM)
    yc = y - mu
    var = _head_sum(yc * yc) * (1.0 / HEAD_DIM)
    y = yc * lax.rsqrt(var + RWKV_GN_EPS) * gng_ref[...] + gnb_ref[...]
    o_rwkv = (y + bv_ref[...]) * g_ref[...]

    d = D_MODEL
    acc = _sigmoid(gates_ref[:, 0:d]) * _dot(ona_ref[...], wb_ref[0])
    acc = acc + _sigmoid(gates_ref[:, d:2 * d]) * _dot(o_rwkv, wb_ref[1])
    acc = acc + _sigmoid(gates_ref[:, 2 * d:3 * d]) * _dot(osgu_ref[...], wb_ref[2])
    z = _dot(acc, wo_ref[...])
    o_ref[...] = _layer_norm(alpha * x_ref[...] + mod_ref[2:3, :] * z, lng_ref[...], lnb_ref[...], LN_EPS)


def _merge(x, mods, gates, o_na, y, g, bv, o_sgu, gn_g, gn_b, w_branch, w_out, ln_g, ln_b, layer,
           n_ctx_tiles, alpha):
    batch, t, d = x.shape
    nt = t // TOKEN_TILE
    w = BRANCH_WIDTH
    row = _mod_row(n_ctx_tiles, batch)
    c2 = lambda b, j: (0, 0)
    tok = lambda n: pl.BlockSpec((None, TOKEN_TILE, n), lambda b, j: (b, j, 0))
    return pl.pallas_call(
        functools.partial(_merge_kernel, alpha=alpha),
        grid=(batch, nt),
        in_specs=[
            tok(d),
            pl.BlockSpec((None, None, 6, d), lambda b, j: (layer, row(b, j), 0, 0)),
            tok(3 * d), tok(w),
            pl.BlockSpec((None, 2, TOKEN_TILE, w), lambda b, j: (b, 0, j, 0)),
            tok(w), tok(w), tok(w),
            pl.BlockSpec((1, w), c2), pl.BlockSpec((1, w), c2),
            pl.BlockSpec((None, 3, w, d), lambda b, j: (layer, 0, 0, 0)),
            pl.BlockSpec((None, d, d), lambda b, j: (layer, 0, 0)),
            pl.BlockSpec((1, d), c2), pl.BlockSpec((1, d), c2),
        ],
        out_specs=tok(d),
        out_shape=jax.ShapeDtypeStruct((batch, t, d), F32),
        compiler_params=_params(("parallel", "parallel")),
        name="merge",
    )(x, mods, gates, o_na, y, g, bv, o_sgu, gn_g, gn_b, w_branch, w_out, ln_g, ln_b)


_FF_CHUNK = 256


def _ffn_kernel(x_ref, mod_ref, wgu_ref, wd_ref, lng_ref, lnb_ref, o_ref, act_ref, *, alpha):
    x = x_ref[...]
    h = (x * (1.0 + mod_ref[4:5, :]) + mod_ref[3:4, :]).astype(BF16)
    for c0 in range(0, D_FF, _FF_CHUNK):
        gate = jnp.dot(h, wgu_ref[:, c0:c0 + _FF_CHUNK], preferred_element_type=F32)
        up = jnp.dot(h, wgu_ref[:, D_FF + c0:D_FF + c0 + _FF_CHUNK], preferred_element_type=F32)
        act_ref[:, c0:c0 + _FF_CHUNK] = (gate * _sigmoid(gate) * up).astype(BF16)
    z = jnp.dot(act_ref[...], wd_ref[...], preferred_element_type=F32)
    o_ref[...] = _layer_norm(alpha * x + mod_ref[5:6, :] * z, lng_ref[...], lnb_ref[...], LN_EPS)


def _ffn(x, mods, w_gu, w_down, ln_g, ln_b, layer, n_ctx_tiles, alpha, skip_tiles):
    batch, t, d = x.shape
    nt = t // TOKEN_TILE - skip_tiles
    row = _mod_row(n_ctx_tiles, batch)
    c2 = lambda b, j: (0, 0)
    return pl.pallas_call(
        functools.partial(_ffn_kernel, alpha=alpha),
        grid=(batch, nt),
        in_specs=[
            pl.BlockSpec((None, TOKEN_TILE, d), lambda b, j: (b, j + skip_tiles, 0)),
            pl.BlockSpec((None, None, 6, d), lambda b, j: (layer, row(b, j + skip_tiles), 0, 0)),
            pl.BlockSpec((None, d, 2 * D_FF), lambda b, j: (layer, 0, 0), pipeline_mode=pl.Buffered(1)),
            pl.BlockSpec((None, D_FF, d), lambda b, j: (layer, 0, 0), pipeline_mode=pl.Buffered(1)),
            pl.BlockSpec((1, d), c2), pl.BlockSpec((1, d), c2),
        ],
        out_specs=pl.BlockSpec((None, TOKEN_TILE, d), lambda b, j: (b, j, 0)),
        out_shape=jax.ShapeDtypeStruct((batch, nt * TOKEN_TILE, d), F32),
        scratch_shapes=[pltpu.VMEM((TOKEN_TILE, D_FF), BF16)],
        compiler_params=_params(("parallel", "parallel")),
        name="ffn",
    )(x, mods, w_gu, w_down, ln_g, ln_b)


def kernel(x, c, ctx, c_ctx, w_ada, b_ada, w_in, na_rpb, rwkv_mu_prev, rwkv_mu_next, rwkv_w0, rwkv_w2, rwkv_a0,
           rwkv_a2, rwkv_g2, rwkv_k_k, rwkv_k_a, rwkv_r_k, rwkv_gn_g, rwkv_gn_b, sgu_ln_g, sgu_ln_b, sgu_w, sgu_b,
           w_branch, w_out, ln1_g, ln1_b, ln2_g, ln2_b, ffn_w_gu, ffn_w_down):
    batch, seq, d = x.shape
    n_ctx = ctx.shape[1]
    depth = w_ada.shape[0]
    assert d == D_MODEL and batch <= 7
    assert n_ctx % TOKEN_TILE == 0 and seq % TOKEN_TILE == 0
    assert seq // GRID_W >= NA_WIN_H
    n_ctx_tiles = n_ctx // TOKEN_TILE
    alpha = float((2 * depth) ** 0.25)

    cond = jnp.zeros((8, d), F32).at[:batch].set(c).at[batch].set(c_ctx)
    mods = _ada_all(cond, w_ada, b_ada).reshape(depth, 8, 6, d)

    xs = jnp.concatenate([ctx, x], axis=1)
    w_in_b = w_in.astype(BF16)
    w_branch_b = w_branch.astype(BF16)
    w_out_b = w_out.astype(BF16)
    w_gu_b = ffn_w_gu.astype(BF16)
    w_down_b = ffn_w_down.astype(BF16)
    sgu_w_b = sgu_w.astype(BF16)
    row = lambda a: a.reshape(1, -1)

    for l in range(depth):
        last = l == depth - 1
        gates, q, k, v, p_rwkv, p_sgu = _inproj(xs, mods, w_in_b, l, n_ctx_tiles)
        o_na = _na_attention(q, k, v, _na_bias_table(na_rpb[l]), n_ctx)
        prm = dict(
            mu_prev=row(rwkv_mu_prev[l]), mu_next=row(rwkv_mu_next[l]),
            w0=rwkv_w0[l], w2=rwkv_w2[l].reshape(2 * DECAY_LORA, BRANCH_WIDTH).astype(BF16),
            a0=rwkv_a0[l], a2=rwkv_a2[l].reshape(2 * DECAY_LORA, BRANCH_WIDTH).astype(BF16),
            g2=rwkv_g2[l].astype(BF16), k_k=row(rwkv_k_k[l]), k_a=row(rwkv_k_a[l]), r_k=row(rwkv_r_k[l]))
        q1, y0, m, n0, g, bv = _rwkv_prep(p_rwkv, prm, n_ctx_tiles)
        y = _rwkv_chain(q1, y0, m, n0, n_ctx)
        b_full = jnp.repeat(jnp.transpose(sgu_b[l]), BRANCH_WIDTH // SGU_GROUPS, axis=1)
        o_sgu = _sgu(p_sgu, row(sgu_ln_g[l]), row(sgu_ln_b[l]), sgu_w_b[l], b_full)
        xs = _merge(xs, mods, gates, o_na, y, g, bv, o_sgu, row(rwkv_gn_g[l]), row(rwkv_gn_b[l]),
                    w_branch_b, w_out_b, row(ln1_g[l]), row(ln1_b[l]), l, n_ctx_tiles, alpha)
        xs = _ffn(xs, mods, w_gu_b, w_down_b, row(ln2_g[l]), row(ln2_b[l]), l, n_ctx_tiles, alpha,
                  n_ctx_tiles if last else 0)
    return xs
```

```python
import functools

import jax
import jax.numpy as jnp
import numpy as np
from jax import lax
from jax.experimental import pallas as pl
from jax.experimental.pallas import tpu as pltpu

F32 = jnp.float32
BF16 = jnp.bfloat16

D_MODEL = 1024
GRID_W = 64
NA_HEADS = 8
HEAD_DIM = 64
NA_WIN_H = 8
NA_WIN_W = 16
BRANCH_WIDTH = 512
DECAY_LORA = 64
GATE_LORA = 128
RWKV_IN = 3 * BRANCH_WIDTH + 4 * DECAY_LORA + GATE_LORA
RWKV_GN_EPS = 64e-5
SGU_CHUNK = 128
SGU_GROUPS = 8
D_FF = 2816
LN_EPS = 1e-5
MASK_VALUE = -1e30

LANES = 128
TOKEN_TILE = 256
RWKV_CHUNK = 64
N_PAIRS = NA_HEADS // 2
QUAD = 4 * HEAD_DIM
HIGHEST = lax.Precision.HIGHEST
VMEM_LIMIT = 56 * 1024 * 1024

_NT = (((1,), (1,)), ((), ()))
_TN = (((0,), (0,)), ((), ()))


def _params(sem, vmem=VMEM_LIMIT):
    return pltpu.CompilerParams(dimension_semantics=sem, vmem_limit_bytes=vmem)


def _dot(a, b):
    return jnp.dot(a.astype(BF16), b.astype(BF16), preferred_element_type=F32)


def _dot_hi(a, b):
    return jnp.dot(a, b, preferred_element_type=F32, precision=HIGHEST)


def _layer_norm(x, g, b, eps):
    mu = jnp.mean(x, axis=-1, keepdims=True)
    xc = x - mu
    var = jnp.mean(xc * xc, axis=-1, keepdims=True)
    return xc * lax.rsqrt(var + eps) * g + b


def _sigmoid(x):
    return 1.0 / (1.0 + jnp.exp(-x))


def _head_sum(x):
    rows = x.shape[0]
    lo = lax.broadcasted_iota(jnp.int32, (rows, LANES), 1) < HEAD_DIM
    out = []
    for p in range(x.shape[1] // LANES):
        xp = x[:, p * LANES:(p + 1) * LANES]
        s_lo = jnp.sum(jnp.where(lo, xp, 0.0), axis=-1, keepdims=True)
        s_hi = jnp.sum(jnp.where(lo, 0.0, xp), axis=-1, keepdims=True)
        out.append(jnp.where(lo, s_lo, s_hi))
    return jnp.concatenate(out, axis=1)


def _ada_kernel(c_ref, w_ref, b_ref, o_ref):
    c = c_ref[...]
    o_ref[...] = _dot(c * _sigmoid(c), w_ref[...]) + b_ref[...]


def _ada_all(cond, w_ada, b_ada):
    depth, d, n = w_ada.shape
    tn = n // 4
    return pl.pallas_call(
        _ada_kernel,
        grid=(depth, n // tn),
        in_specs=[
            pl.BlockSpec((8, d), lambda l, i: (0, 0)),
            pl.BlockSpec((None, d, tn), lambda l, i: (l, 0, i)),
            pl.BlockSpec((None, 1, tn), lambda l, i: (l, 0, i)),
        ],
        out_specs=pl.BlockSpec((None, 8, tn), lambda l, i: (l, 0, i)),
        out_shape=jax.ShapeDtypeStruct((depth, 8, n), F32),
        compiler_params=_params(("parallel", "parallel")),
        name="adaln",
    )(cond, w_ada, b_ada.reshape(depth, 1, n))


_IN_SPLITS = (0, 3 * D_MODEL, 3 * D_MODEL + 512, 3 * D_MODEL + 1024, 3 * D_MODEL + 1536,
              3 * D_MODEL + 1536 + RWKV_IN, 3 * D_MODEL + 1536 + RWKV_IN + 2 * BRANCH_WIDTH)
_IN_COL_CHUNK = 512


def _inproj_kernel(x_ref, mod_ref, w_ref, gates_ref, q_ref, k_ref, v_ref, pr_ref, ps_ref):
    h = (x_ref[...] * (1.0 + mod_ref[1:2, :]) + mod_ref[0:1, :]).astype(BF16)
    outs = (gates_ref, q_ref, k_ref, v_ref, pr_ref, ps_ref)
    for idx, o_ref in enumerate(outs):
        lo, hi = _IN_SPLITS[idx], _IN_SPLITS[idx + 1]
        scale = HEAD_DIM ** -0.5 if idx == 1 else None
        for c0 in range(lo, hi, _IN_COL_CHUNK):
            c1 = min(c0 + _IN_COL_CHUNK, hi)
            r = jnp.dot(h, w_ref[:, c0:c1], preferred_element_type=F32)
            if scale is not None:
                r = r * scale
            o_ref[:, c0 - lo:c1 - lo] = r.astype(o_ref.dtype)


def _mod_row(n_ctx_tiles, batch):
    return lambda b, j: jnp.where(j < n_ctx_tiles, batch, b)


def _inproj(x, mods, w_in, layer, n_ctx_tiles):
    batch, t, d = x.shape
    nt = t // TOKEN_TILE
    d_in = w_in.shape[-1]
    row = _mod_row(n_ctx_tiles, batch)
    widths = [_IN_SPLITS[i + 1] - _IN_SPLITS[i] for i in range(6)]
    dtypes = [F32, BF16, BF16, BF16, F32, F32]
    return pl.pallas_call(
        _inproj_kernel,
        grid=(batch, nt),
        in_specs=[
            pl.BlockSpec((None, TOKEN_TILE, d), lambda b, j: (b, j, 0)),
            pl.BlockSpec((None, None, 6, d), lambda b, j: (layer, row(b, j), 0, 0)),
            pl.BlockSpec((None, d, d_in), lambda b, j: (layer, 0, 0), pipeline_mode=pl.Buffered(1)),
        ],
        out_specs=[pl.BlockSpec((None, TOKEN_TILE, w), lambda b, j: (b, j, 0)) for w in widths],
        out_shape=[jax.ShapeDtypeStruct((batch, t, w), dt) for w, dt in zip(widths, dtypes)],
        compiler_params=_params(("parallel", "parallel")),
        name="inproj",
    )(x, mods, w_in)


def _na_bias_table(rpb):
    cols = np.arange(GRID_W)
    col0 = np.clip(cols - NA_WIN_W // 2, 0, GRID_W - NA_WIN_W)
    kc = np.arange(GRID_W)
    in_win = (kc[None, :] >= col0[:, None]) & (kc[None, :] < col0[:, None] + NA_WIN_W)
    dcol = kc[None, :] - cols[:, None] + NA_WIN_W - 1
    n_dcol = 2 * NA_WIN_W - 1
    onehot = ((dcol[None] == np.arange(n_dcol)[:, None, None]) & in_win[None]).astype(np.float32)
    toe = jnp.einsum("hrd,dck->hrck", rpb, jnp.asarray(onehot), precision=HIGHEST)
    toe = toe + jnp.asarray(np.where(in_win, 0.0, MASK_VALUE).astype(np.float32))
    tab = jnp.stack([toe[:, NA_WIN_H - 1 - e:2 * NA_WIN_H - 1 - e] for e in range(NA_WIN_H)])
    tab = jnp.transpose(tab, (0, 1, 3, 2, 4))
    return tab.reshape(NA_WIN_H, NA_HEADS, GRID_W, NA_WIN_H * GRID_W)


NA_ROWS_PER_STEP = 4


def _na_kernel(q_ref, k_ref, v_ref, *rest, n_ctx, rows):
    bias_refs, o_ref = rest[:NA_ROWS_PER_STEP], rest[NA_ROWS_PER_STEP]
    j = pl.program_id(1)
    ctx_steps = n_ctx // (GRID_W * NA_ROWS_PER_STEP)
    lo2 = lax.broadcasted_iota(jnp.int32, (2 * GRID_W, LANES), 1) < HEAD_DIM
    top = lax.broadcasted_iota(jnp.int32, (2 * GRID_W, LANES), 0) < GRID_W
    own_head = lo2 == top
    lo = lax.broadcasted_iota(jnp.int32, (GRID_W, LANES), 1) < HEAD_DIM
    n_win = NA_WIN_H * GRID_W
    pairs = [slice(p * LANES, (p + 1) * LANES) for p in range(N_PAIRS)]

    def attend(i, local):
        qs = slice(i * GRID_W, (i + 1) * GRID_W)
        if local:
            r = (j - ctx_steps) * NA_ROWS_PER_STEP + i
            row0 = jnp.clip(r - NA_WIN_H // 2, 0, rows - NA_WIN_H)
            start = pl.multiple_of(n_ctx + row0 * GRID_W, GRID_W)
        q2 = []
        for cs in pairs:
            qp = q_ref[qs, cs].astype(F32)
            q2.append(jnp.where(own_head, jnp.concatenate([qp, qp], axis=0), 0.0).astype(BF16))
        s_c = [lax.dot_general(q_, k_ref[0:n_ctx, cs], _NT, preferred_element_type=F32)
               for q_, cs in zip(q2, pairs)]
        mx = [jnp.max(s, axis=-1, keepdims=True) for s in s_c]
        if local:
            s_l = [lax.dot_general(q_, k_ref[pl.ds(start, n_win), cs], _NT, preferred_element_type=F32)
                   + bias_refs[i][p] for p, (q_, cs) in enumerate(zip(q2, pairs))]
            mx = [jnp.maximum(m, jnp.max(s, axis=-1, keepdims=True)) for m, s in zip(mx, s_l)]
            p_l = [jnp.exp(s - m) for s, m in zip(s_l, mx)]
        p_c = [jnp.exp(s - m) for s, m in zip(s_c, mx)]
        den = [jnp.sum(x, axis=-1, keepdims=True) for x in p_c]
        acc = [jnp.dot(x.astype(BF16), v_ref[0:n_ctx, cs], preferred_element_type=F32)
               for x, cs in zip(p_c, pairs)]
        if local:
            den = [d_ + jnp.sum(x, axis=-1, keepdims=True) for d_, x in zip(den, p_l)]
            acc = [a + jnp.dot(x.astype(BF16), v_ref[pl.ds(start, n_win), cs], preferred_element_type=F32)
                   for a, x, cs in zip(acc, p_l, pairs)]
        for a, d_, cs in zip(acc, den, pairs):
            o = a / d_
            o_ref[qs, cs] = jnp.where(lo, o[:GRID_W], o[GRID_W:])

    @pl.when(j < ctx_steps)
    def _():
        for i in range(NA_ROWS_PER_STEP):
            attend(i, False)

    @pl.when(j >= ctx_steps)
    def _():
        for i in range(NA_ROWS_PER_STEP):
            attend(i, True)


def _na_attention(q, k, v, bias, n_ctx):
    batch, t, w = q.shape
    rows = (t - n_ctx) // GRID_W
    tq = GRID_W * NA_ROWS_PER_STEP
    assert n_ctx % tq == 0 and rows % NA_ROWS_PER_STEP == 0
    ctx_steps = n_ctx // tq
    bias = bias.reshape(NA_WIN_H, N_PAIRS, 2 * GRID_W, NA_WIN_H * GRID_W)

    def bias_spec(i):
        def idx(b, j):
            r = jnp.maximum(j - ctx_steps, 0) * NA_ROWS_PER_STEP + i
            row0 = jnp.clip(r - NA_WIN_H // 2, 0, rows - NA_WIN_H)
            return (r - row0, 0, 0, 0)
        return pl.BlockSpec((None, N_PAIRS, 2 * GRID_W, NA_WIN_H * GRID_W), idx)

    return pl.pallas_call(
        functools.partial(_na_kernel, n_ctx=n_ctx, rows=rows),
        grid=(batch, t // tq),
        in_specs=[
            pl.BlockSpec((None, tq, w), lambda b, j: (b, j, 0)),
            pl.BlockSpec((None, t, w), lambda b, j: (b, 0, 0)),
            pl.BlockSpec((None, t, w), lambda b, j: (b, 0, 0)),
        ] + [bias_spec(i) for i in range(NA_ROWS_PER_STEP)],
        out_specs=pl.BlockSpec((None, tq, w), lambda b, j: (b, j, 0)),
        out_shape=jax.ShapeDtypeStruct((batch, t, w), F32),
        compiler_params=_params(("parallel", "arbitrary")),
        name="na_attention",
    )(q, k, v, *([bias] * NA_ROWS_PER_STEP))


def _gelu_tanh(x):
    return x * (0.5 * (1.0 + jnp.tanh(0.7978845608028654 * (x + 0.044715 * (x * x * x)))))


def _sgu_kernel(p_ref, lng_ref, lnb_ref, w_ref, b_ref, o_ref):
    g = _gelu_tanh(p_ref[...])
    u = g[:, :BRANCH_WIDTH]
    v = _layer_norm(g[:, BRANCH_WIDTH:], lng_ref[...], lnb_ref[...], LN_EPS).astype(BF16)
    lo = lax.broadcasted_iota(jnp.int32, (SGU_CHUNK, LANES), 1) < HEAD_DIM
    for p in range(N_PAIRS):
        cs = slice(p * LANES, (p + 1) * LANES)
        vp = v[:, cs]
        r0 = jnp.dot(w_ref[2 * p], vp, preferred_element_type=F32)
        r1 = jnp.dot(w_ref[2 * p + 1], vp, preferred_element_type=F32)
        o_ref[:, cs] = u[:, cs] * (jnp.where(lo, r0, r1) + b_ref[:, cs])


def _sgu(p_sgu, ln_g, ln_b, w_s, b_full):
    batch, t, w = p_sgu.shape
    const2 = lambda b, j: (0, 0)
    return pl.pallas_call(
        _sgu_kernel,
        grid=(batch, t // SGU_CHUNK),
        in_specs=[
            pl.BlockSpec((None, SGU_CHUNK, w), lambda b, j: (b, j, 0)),
            pl.BlockSpec((1, BRANCH_WIDTH), const2),
            pl.BlockSpec((1, BRANCH_WIDTH), const2),
            pl.BlockSpec((SGU_GROUPS, SGU_CHUNK, SGU_CHUNK), lambda b, j: (0, 0, 0)),
            pl.BlockSpec((SGU_CHUNK, BRANCH_WIDTH), const2),
        ],
        out_specs=pl.BlockSpec((None, SGU_CHUNK, BRANCH_WIDTH), lambda b, j: (b, j, 0)),
        out_shape=jax.ShapeDtypeStruct((batch, t, BRANCH_WIDTH), F32),
        compiler_params=_params(("parallel", "parallel")),
        name="sgu",
    )(p_sgu, ln_g, ln_b, w_s, b_full)


_N_STAGE = 8


def _rwkv_prep_kernel(p_ref, prev_ref, next_ref, mup_ref, mun_ref, w0_ref, w2_ref, a0_ref, a2_ref, g2_ref,
                      kk_ref, ka_ref, rk_ref,
                      q1_ref, y0_ref, m_ref, n0_ref, g_ref, bv_ref, stage_ref, *, n_ctx_tiles, n_tiles):
    j = pl.program_id(1)
    tm = TOKEN_TILE
    w = BRANCH_WIDTH
    ch = RWKV_CHUNK

    p = p_ref[...]
    first = (j == 0) | (j == n_ctx_tiles)
    last = (j == n_ctx_tiles - 1) | (j == n_tiles - 1)
    prev_row = jnp.where(first, 0.0, prev_ref[7:8, :])
    next_row = jnp.where(last, 0.0, next_ref[0:1, :])
    row = lax.broadcasted_iota(jnp.int32, (tm, 1), 0)
    p_prev = jnp.where(row == 0, prev_row, pltpu.roll(p, 1, 0))
    p_next = jnp.where(row == tm - 1, next_row, pltpu.roll(p, tm - 1, 0))
    ps = p + mup_ref[...] * (p_prev - p) + mun_ref[...] * (p_next - p)

    r = ps[:, 0:w]
    k = ps[:, w:2 * w]
    v = ps[:, 2 * w:3 * w]
    wc = jnp.tanh(ps[:, 3 * w:3 * w + LANES])
    ac = ps[:, 3 * w + LANES:3 * w + 2 * LANES]
    gc = ps[:, 3 * w + 2 * LANES:3 * w + 3 * LANES]

    g_ref[...] = _dot(_sigmoid(gc), g2_ref[...])
    kk = k * kk_ref[...]
    kk = kk * lax.rsqrt(_head_sum(kk * kk) + 1e-12)

    lo_dir = lax.broadcasted_iota(jnp.int32, (tm, LANES), 1) < DECAY_LORA
    ti = lax.broadcasted_iota(jnp.int32, (tm, tm), 0)
    si = lax.broadcasted_iota(jnp.int32, (tm, tm), 1)
    same_chunk = (ti // ch) == (si // ch)
    ones_blk = jnp.where(same_chunk, 1.0, 0.0)
    ci = lax.broadcasted_iota(jnp.int32, (ch, QUAD), 0)
    cj = lax.broadcasted_iota(jnp.int32, (ch, QUAD), 1) % HEAD_DIM
    eye_w = jnp.where(ci == cj, 1.0, 0.0)
    head_of_lane = lax.broadcasted_iota(jnp.int32, (ch, QUAD), 1) // HEAD_DIM
    head_masks = [head_of_lane == h for h in range(QUAD // HEAD_DIM)]
    bi = lax.broadcasted_iota(jnp.int32, (QUAD, QUAD), 0) // HEAD_DIM
    bj = lax.broadcasted_iota(jnp.int32, (QUAD, QUAD), 1) // HEAD_DIM
    maskb = jnp.where(bi == bj, 1.0, 0.0).astype(BF16)
    zeros_w = jnp.zeros((ch, QUAD), F32)

    kd_sum = jnp.zeros((tm, w), F32)
    for d in range(2):
        md = lo_dir if d == 0 else ~lo_dir
        z = w0_ref[d:d + 1, :] + _dot(jnp.where(md, wc, 0.0), w2_ref[...])
        w_log = -(jnp.maximum(-z, 0.0) + jnp.log(1.0 + jnp.exp(-jnp.abs(z)))) - 0.5
        lw = -jnp.exp(w_log)
        a = _sigmoid(a0_ref[d:d + 1, :] + _dot(jnp.where(md, ac, 0.0), a2_ref[...]))
        kd = k * (1.0 + (a - 1.0) * ka_ref[...])
        kd_sum = kd_sum + kd
        beta = kk * a
        before = (si <= ti) if d == 0 else (si >= ti)
        tri = jnp.where(same_chunk & before, 1.0, 0.0)
        cum = _dot_hi(tri, lw)
        tot = _dot_hi(ones_blk, lw)
        e_neg = jnp.exp(-cum)
        e_hat = jnp.exp(tot - cum)
        stage_ref[0] = -kk * jnp.exp(cum - lw)
        stage_ref[1] = r * jnp.exp(cum)
        stage_ref[2] = beta * e_neg
        stage_ref[3] = kd * e_neg
        stage_ref[4] = beta * e_hat
        stage_ref[5] = kd * e_hat
        stage_ref[6] = v
        stage_ref[7] = jnp.exp(tot)
        strict = (cj < ci) if d == 0 else (cj > ci)
        incl = (cj <= ci) if d == 0 else (cj >= ci)

        def bd(x):
            return jnp.concatenate([x.astype(BF16)] * (QUAD // HEAD_DIM), axis=0) * maskb

        def wide_dot(lhs, rhs):
            return jnp.dot(lhs.astype(BF16), rhs, preferred_element_type=F32)

        def diag_blocks(full):
            out = zeros_w
            for h, hm in enumerate(head_masks):
                out = out + jnp.where(hm, full[h * HEAD_DIM:(h + 1) * HEAD_DIM], 0.0)
            return out

        chains = [(c, qd) for c in range(tm // ch) for qd in range(w // QUAD)]

        def load(i):
            return [stage_ref[i, c * ch:(c + 1) * ch, qd * QUAD:(qd + 1) * QUAD] for c, qd in chains]

        ab, rb, bt, kt = load(0), load(1), load(2), load(3)
        lhs = [jnp.concatenate([a_, r_], axis=0).astype(BF16) for a_, r_ in zip(ab, rb)]
        g1 = [lax.dot_general(l_, bd(x), _NT, preferred_element_type=F32) for l_, x in zip(lhs, bt)]
        g2 = [lax.dot_general(l_, bd(x), _NT, preferred_element_type=F32) for l_, x in zip(lhs, kt)]
        a_ab = [jnp.where(strict, g[:ch], 0.0) for g in g1]
        a_rb = [jnp.where(incl, g[ch:], 0.0) for g in g1]
        a_kk = [jnp.concatenate([jnp.where(strict, g[:ch], 0.0), jnp.where(incl, g[ch:], 0.0)], axis=0)
                for g in g2]
        tinv = [eye_w + a for a in a_ab]
        apow = [wide_dot(a, bd(a)) for a in a_ab]
        for _ in range(4):
            pt = [wide_dot(jnp.concatenate([p_, t_], axis=0), bd(p_)) for p_, t_ in zip(apow, tinv)]
            apow = [x[:ch] for x in pt]
            tinv = [t_ + x[ch:] for t_, x in zip(tinv, pt)]
        tinv = [t_ + wide_dot(t_, bd(p_)) for t_, p_ in zip(tinv, apow)]
        vv = load(6)
        zv = [wide_dot(a, bd(x)) for a, x in zip(a_kk, vv)]
        wm = [wide_dot(t_, jnp.concatenate([bd(a_), bd(z[:ch])], axis=1))
              for t_, a_, z in zip(tinv, ab, zv)]
        qy = [wide_dot(a, jnp.concatenate([bd(x[:, :QUAD]), bd(x[:, QUAD:])], axis=1))
              for a, x in zip(a_rb, wm)]
        bh, kh = load(4), load(5)
        for i, (c, qd) in enumerate(chains):
            rs = slice(c * ch, (c + 1) * ch)
            cs = slice(qd * QUAD, (qd + 1) * QUAD)
            q1_ref[d, rs, cs] = rb[i] + qy[i][:, :QUAD]
            y0_ref[d, rs, cs] = qy[i][:, QUAD:] + zv[i][ch:]
            lhs_t = jnp.concatenate([bh[i], kh[i]], axis=0).astype(BF16)
            rhs = jnp.concatenate([wm[i], jnp.concatenate([zeros_w, vv[i]], axis=1)], axis=0).astype(BF16)
            mn = lax.dot_general(lhs_t, rhs, _TN, preferred_element_type=F32)
            dg = stage_ref[7, c * ch:c * ch + 1, cs]
            m_ref[d, c, :, cs] = diag_blocks(mn[:, :QUAD]) + jnp.where(eye_w > 0.0, dg, 0.0)
            n0_ref[d, c, :, cs] = diag_blocks(mn[:, QUAD:])

    bv_ref[...] = _head_sum(r * kd_sum * rk_ref[...]) * v


def _rwkv_prep(p_rwkv, prm, n_ctx_tiles):
    batch, t, wp = p_rwkv.shape
    nt = t // TOKEN_TILE
    cpt = TOKEN_TILE // RWKV_CHUNK
    nc = t // RWKV_CHUNK
    halo = TOKEN_TILE // 8
    w = BRANCH_WIDTH
    c2 = lambda b, j: (0, 0)
    vec = lambda n: pl.BlockSpec((1, n), c2)
    tok = pl.BlockSpec((None, TOKEN_TILE, w), lambda b, j: (b, j, 0))
    dtok = pl.BlockSpec((None, 2, TOKEN_TILE, w), lambda b, j: (b, 0, j, 0))
    mat = pl.BlockSpec((None, 2, cpt, HEAD_DIM, w), lambda b, j: (b, 0, j, 0, 0))
    return pl.pallas_call(
        functools.partial(_rwkv_prep_kernel, n_ctx_tiles=n_ctx_tiles, n_tiles=nt),
        grid=(batch, nt),
        in_specs=[
            pl.BlockSpec((None, TOKEN_TILE, wp), lambda b, j: (b, j, 0)),
            pl.BlockSpec((None, 8, wp), lambda b, j: (b, jnp.maximum(j * halo - 1, 0), 0)),
            pl.BlockSpec((None, 8, wp), lambda b, j: (b, jnp.minimum((j + 1) * halo, t // 8 - 1), 0)),
            vec(wp), vec(wp),
            pl.BlockSpec((2, w), c2), pl.BlockSpec((2 * DECAY_LORA, w), c2),
            pl.BlockSpec((2, w), c2), pl.BlockSpec((2 * DECAY_LORA, w), c2),
            pl.BlockSpec((GATE_LORA, w), c2),
            vec(w), vec(w), vec(w),
        ],
        out_specs=[dtok, dtok, mat, mat, tok, tok],
        out_shape=[
            jax.ShapeDtypeStruct((batch, 2, t, w), F32),
            jax.ShapeDtypeStruct((batch, 2, t, w), F32),
            jax.ShapeDtypeStruct((batch, 2, nc, HEAD_DIM, w), F32),
            jax.ShapeDtypeStruct((batch, 2, nc, HEAD_DIM, w), F32),
            jax.ShapeDtypeStruct((batch, t, w), F32),
            jax.ShapeDtypeStruct((batch, t, w), F32),
        ],
        scratch_shapes=[pltpu.VMEM((_N_STAGE, TOKEN_TILE, w), F32)],
        compiler_params=_params(("parallel", "parallel")),
        name="rwkv_prep",
    )(p_rwkv, p_rwkv, p_rwkv, prm["mu_prev"], prm["mu_next"], prm["w0"], prm["w2"], prm["a0"], prm["a2"],
      prm["g2"], prm["k_k"], prm["k_a"], prm["r_k"])


def _rwkv_chain_kernel(q1f_ref, y0f_ref, mf_ref, n0f_ref, q1b_ref, y0b_ref, mb_ref, n0b_ref,
                       yf_ref, yb_ref, s_ref):
    @pl.when(pl.program_id(0) == 0)
    def _():
        s_ref[...] = jnp.zeros_like(s_ref)

    batch = q1f_ref.shape[0]
    bi = lax.broadcasted_iota(jnp.int32, (QUAD, QUAD), 0) // HEAD_DIM
    bj = lax.broadcasted_iota(jnp.int32, (QUAD, QUAD), 1) // HEAD_DIM
    blk = bi == bj
    dirs = ((q1f_ref, y0f_ref, mf_ref, n0f_ref, yf_ref), (q1b_ref, y0b_ref, mb_ref, n0b_ref, yb_ref))
    chains = [(d, b, slice(qd * QUAD, (qd + 1) * QUAD))
              for d in range(2) for b in range(batch) for qd in range(BRANCH_WIDTH // QUAD)]
    outs = []
    for d, b, cs in chains:
        q1_ref, _, m_ref, _, _ = dirs[d]
        s = s_ref[d, b, :, cs]
        s_bd = jnp.where(blk, jnp.concatenate([s] * (QUAD // HEAD_DIM), axis=0), 0.0)
        outs.append(_dot_hi(jnp.concatenate([q1_ref[b, :, cs], m_ref[b, :, cs]], axis=0), s_bd))
    for (d, b, cs), o in zip(chains, outs):
        _, y0_ref, _, n0_ref, y_ref = dirs[d]
        y_ref[b, :, cs] = o[:RWKV_CHUNK] + y0_ref[b, :, cs]
        s_ref[d, b, :, cs] = o[RWKV_CHUNK:] + n0_ref[b, :, cs]


def _rwkv_chain(q1, y0, m, n0, n_ctx):
    batch, _, t, w = q1.shape
    nc = t // RWKV_CHUNK
    ncc = n_ctx // RWKV_CHUNK

    def chunk(d, s):
        return s if d == 0 else jnp.where(s < ncc, ncc - 1 - s, nc - 1 - (s - ncc))

    in_specs = []
    for d in range(2):
        tok = pl.BlockSpec((batch, None, RWKV_CHUNK, w), lambda s, d=d: (0, d, chunk(d, s), 0))
        mat = pl.BlockSpec((batch, None, None, HEAD_DIM, w), lambda s, d=d: (0, d, chunk(d, s), 0, 0))
        in_specs += [tok, tok, mat, mat]
    out_specs = [pl.BlockSpec((batch, RWKV_CHUNK, w), lambda s, d=d: (0, chunk(d, s), 0)) for d in range(2)]
    return pl.pallas_call(
        _rwkv_chain_kernel,
        grid=(nc,),
        in_specs=in_specs,
        out_specs=out_specs,
        out_shape=[jax.ShapeDtypeStruct((batch, t, w), F32)] * 2,
        scratch_shapes=[pltpu.VMEM((2, batch, HEAD_DIM, w), F32)],
        compiler_params=_params(("arbitrary",)),
        name="rwkv_chain",
    )(q1, y0, m, n0, q1, y0, m, n0)


def _merge_kernel(x_ref, mod_ref, gates_ref, ona_ref, yf_ref, yb_ref, g_ref, bv_ref, osgu_ref, gng_ref, gnb_ref,
                  wb_ref, wo_ref, lng_ref, lnb_ref, o_ref, *, alpha):
    y = yf_ref[...] + yb_ref[...]
    mu = _head_sum(y) * (1.0 / HEAD_DIM)
    yc = y - mu
    var = _head_sum(yc * yc) * (1.0 / HEAD_DIM)
    y = yc * lax.rsqrt(var + RWKV_GN_EPS) * gng_ref[...] + gnb_ref[...]
    o_rwkv = (y + bv_ref[...]) * g_ref[...]

    d = D_MODEL
    acc = _sigmoid(gates_ref[:, 0:d]) * _dot(ona_ref[...], wb_ref[0])
    acc = acc + _sigmoid(gates_ref[:, d:2 * d]) * _dot(o_rwkv, wb_ref[1])
    acc = acc + _sigmoid(gates_ref[:, 2 * d:3 * d]) * _dot(osgu_ref[...], wb_ref[2])
    z = _dot(acc, wo_ref[...])
    o_ref[...] = _layer_norm(alpha * x_ref[...] + mod_ref[2:3, :] * z, lng_ref[...], lnb_ref[...], LN_EPS)


def _merge(x, mods, gates, o_na, y_f, y_b, g, bv, o_sgu, gn_g, gn_b, w_branch, w_out, ln_g, ln_b, layer,
           n_ctx_tiles, alpha):
    batch, t, d = x.shape
    nt = t // TOKEN_TILE
    w = BRANCH_WIDTH
    row = _mod_row(n_ctx_tiles, batch)
    c2 = lambda b, j: (0, 0)
    tok = lambda n: pl.BlockSpec((None, TOKEN_TILE, n), lambda b, j: (b, j, 0))
    return pl.pallas_call(
        functools.partial(_merge_kernel, alpha=alpha),
        grid=(batch, nt),
        in_specs=[
            tok(d),
            pl.BlockSpec((None, None, 6, d), lambda b, j: (layer, row(b, j), 0, 0)),
            tok(3 * d), tok(w),
            tok(w), tok(w),
            tok(w), tok(w), tok(w),
            pl.BlockSpec((1, w), c2), pl.BlockSpec((1, w), c2),
            pl.BlockSpec((None, 3, w, d), lambda b, j: (layer, 0, 0, 0)),
            pl.BlockSpec((None, d, d), lambda b, j: (layer, 0, 0)),
            pl.BlockSpec((1, d), c2), pl.BlockSpec((1, d), c2),
        ],
        out_specs=tok(d),
        out_shape=jax.ShapeDtypeStruct((batch, t, d), F32),
        compiler_params=_params(("parallel", "parallel")),
        name="merge",
    )(x, mods, gates, o_na, y_f, y_b, g, bv, o_sgu, gn_g, gn_b, w_branch, w_out, ln_g, ln_b)


_FF_CHUNK = 256


def _ffn_kernel(x_ref, mod_ref, wgu_ref, wd_ref, lng_ref, lnb_ref, o_ref, act_ref, *, alpha):
    x = x_ref[...]
    h = (x * (1.0 + mod_ref[4:5, :]) + mod_ref[3:4, :]).astype(BF16)
    for c0 in range(0, D_FF, _FF_CHUNK):
        gate = jnp.dot(h, wgu_ref[:, c0:c0 + _FF_CHUNK], preferred_element_type=F32)
        up = jnp.dot(h, wgu_ref[:, D_FF + c0:D_FF + c0 + _FF_CHUNK], preferred_element_type=F32)
        act_ref[:, c0:c0 + _FF_CHUNK] = (gate * _sigmoid(gate) * up).astype(BF16)
    z = jnp.dot(act_ref[...], wd_ref[...], preferred_element_type=F32)
    o_ref[...] = _layer_norm(alpha * x + mod_ref[5:6, :] * z, lng_ref[...], lnb_ref[...], LN_EPS)


def _ffn(x, mods, w_gu, w_down, ln_g, ln_b, layer, n_ctx_tiles, alpha, skip_tiles):
    batch, t, d = x.shape
    nt = t // TOKEN_TILE - skip_tiles
    row = _mod_row(n_ctx_tiles, batch)
    c2 = lambda b, j: (0, 0)
    return pl.pallas_call(
        functools.partial(_ffn_kernel, alpha=alpha),
        grid=(batch, nt),
        in_specs=[
            pl.BlockSpec((None, TOKEN_TILE, d), lambda b, j: (b, j + skip_tiles, 0)),
            pl.BlockSpec((None, None, 6, d), lambda b, j: (layer, row(b, j + skip_tiles), 0, 0)),
            pl.BlockSpec((None, d, 2 * D_FF), lambda b, j: (layer, 0, 0), pipeline_mode=pl.Buffered(1)),
            pl.BlockSpec((None, D_FF, d), lambda b, j: (layer, 0, 0), pipeline_mode=pl.Buffered(1)),
            pl.BlockSpec((1, d), c2), pl.BlockSpec((1, d), c2),
        ],
        out_specs=pl.BlockSpec((None, TOKEN_TILE, d), lambda b, j: (b, j, 0)),
        out_shape=jax.ShapeDtypeStruct((batch, nt * TOKEN_TILE, d), F32),
        scratch_shapes=[pltpu.VMEM((TOKEN_TILE, D_FF), BF16)],
        compiler_params=_params(("parallel", "parallel")),
        name="ffn",
    )(x, mods, w_gu, w_down, ln_g, ln_b)


def kernel(x, c, ctx, c_ctx, w_ada, b_ada, w_in, na_rpb, rwkv_mu_prev, rwkv_mu_next, rwkv_w0, rwkv_w2, rwkv_a0,
           rwkv_a2, rwkv_g2, rwkv_k_k, rwkv_k_a, rwkv_r_k, rwkv_gn_g, rwkv_gn_b, sgu_ln_g, sgu_ln_b, sgu_w, sgu_b,
           w_branch, w_out, ln1_g, ln1_b, ln2_g, ln2_b, ffn_w_gu, ffn_w_down):
    batch, seq, d = x.shape
    n_ctx = ctx.shape[1]
    depth = w_ada.shape[0]
    assert d == D_MODEL and batch <= 7
    assert n_ctx % TOKEN_TILE == 0 and seq % TOKEN_TILE == 0
    assert seq // GRID_W >= NA_WIN_H
    n_ctx_tiles = n_ctx // TOKEN_TILE
    alpha = float((2 * depth) ** 0.25)

    cond = jnp.zeros((8, d), F32).at[:batch].set(c).at[batch].set(c_ctx)
    mods = _ada_all(cond, w_ada, b_ada).reshape(depth, 8, 6, d)

    xs = jnp.concatenate([ctx, x], axis=1)
    w_in_b = w_in.astype(BF16)
    w_branch_b = w_branch.astype(BF16)
    w_out_b = w_out.astype(BF16)
    w_gu_b = ffn_w_gu.astype(BF16)
    w_down_b = ffn_w_down.astype(BF16)
    sgu_w_b = sgu_w.astype(BF16)
    row = lambda a: a.reshape(1, -1)

    for l in range(depth):
        last = l == depth - 1
        gates, q, k, v, p_rwkv, p_sgu = _inproj(xs, mods, w_in_b, l, n_ctx_tiles)
        o_na = _na_attention(q, k, v, _na_bias_table(na_rpb[l]), n_ctx)
        prm = dict(
            mu_prev=row(rwkv_mu_prev[l]), mu_next=row(rwkv_mu_next[l]),
            w0=rwkv_w0[l], w2=rwkv_w2[l].reshape(2 * DECAY_LORA, BRANCH_WIDTH).astype(BF16),
            a0=rwkv_a0[l], a2=rwkv_a2[l].reshape(2 * DECAY_LORA, BRANCH_WIDTH).astype(BF16),
            g2=rwkv_g2[l].astype(BF16), k_k=row(rwkv_k_k[l]), k_a=row(rwkv_k_a[l]), r_k=row(rwkv_r_k[l]))
        q1, y0, m, n0, g, bv = _rwkv_prep(p_rwkv, prm, n_ctx_tiles)
        y_f, y_b = _rwkv_chain(q1, y0, m, n0, n_ctx)
        b_full = jnp.repeat(jnp.transpose(sgu_b[l]), BRANCH_WIDTH // SGU_GROUPS, axis=1)
        o_sgu = _sgu(p_sgu, row(sgu_ln_g[l]), row(sgu_ln_b[l]), sgu_w_b[l], b_full)
        xs = _merge(xs, mods, gates, o_na, y_f, y_b, g, bv, o_sgu, row(rwkv_gn_g[l]), row(rwkv_gn_b[l]),
                    w_branch_b, w_out_b, row(ln1_g[l]), row(ln1_b[l]), l, n_ctx_tiles, alpha)
        xs = _ffn(xs, mods, w_gu_b, w_down_b, row(ln2_g[l]), row(ln2_b[l]), l, n_ctx_tiles, alpha,
                  n_ctx_tiles if last else 0)
    return xs
```

```python
import functools

import jax
import jax.numpy as jnp
import numpy as np
from jax import lax
from jax.experimental import pallas as pl
from jax.experimental.pallas import tpu as pltpu

F32 = jnp.float32
BF16 = jnp.bfloat16

D_MODEL = 1024
GRID_W = 64
NA_HEADS = 8
HEAD_DIM = 64
NA_WIN_H = 8
NA_WIN_W = 16
BRANCH_WIDTH = 512
DECAY_LORA = 64
GATE_LORA = 128
RWKV_IN = 3 * BRANCH_WIDTH + 4 * DECAY_LORA + GATE_LORA
RWKV_GN_EPS = 64e-5
SGU_CHUNK = 128
SGU_GROUPS = 8
D_FF = 2816
LN_EPS = 1e-5
MASK_VALUE = -1e30

LANES = 128
TOKEN_TILE = 256
RWKV_CHUNK = 64
N_PAIRS = NA_HEADS // 2
QUAD = 4 * HEAD_DIM
HIGHEST = lax.Precision.HIGHEST
VMEM_LIMIT = 56 * 1024 * 1024

_NT = (((1,), (1,)), ((), ()))
_TN = (((0,), (0,)), ((), ()))


def _params(sem, vmem=VMEM_LIMIT):
    return pltpu.CompilerParams(dimension_semantics=sem, vmem_limit_bytes=vmem)


def _dot(a, b):
    return jnp.dot(a.astype(BF16), b.astype(BF16), preferred_element_type=F32)


def _dot_hi(a, b):
    return jnp.dot(a, b, preferred_element_type=F32, precision=HIGHEST)


def _layer_norm(x, g, b, eps):
    mu = jnp.mean(x, axis=-1, keepdims=True)
    xc = x - mu
    var = jnp.mean(xc * xc, axis=-1, keepdims=True)
    return xc * lax.rsqrt(var + eps) * g + b


def _sigmoid(x):
    return 1.0 / (1.0 + jnp.exp(-x))


def _head_sum(x):
    rows = x.shape[0]
    lo = lax.broadcasted_iota(jnp.int32, (rows, LANES), 1) < HEAD_DIM
    out = []
    for p in range(x.shape[1] // LANES):
        xp = x[:, p * LANES:(p + 1) * LANES]
        s_lo = jnp.sum(jnp.where(lo, xp, 0.0), axis=-1, keepdims=True)
        s_hi = jnp.sum(jnp.where(lo, 0.0, xp), axis=-1, keepdims=True)
        out.append(jnp.where(lo, s_lo, s_hi))
    return jnp.concatenate(out, axis=1)


def _ada_kernel(c_ref, w_ref, b_ref, o_ref):
    c = c_ref[...]
    o_ref[...] = _dot(c * _sigmoid(c), w_ref[...]) + b_ref[...]


def _ada_all(cond, w_ada, b_ada):
    depth, d, n = w_ada.shape
    n_mod = n // d
    return pl.pallas_call(
        _ada_kernel,
        grid=(depth, n_mod),
        in_specs=[
            pl.BlockSpec((8, d), lambda l, i: (0, 0)),
            pl.BlockSpec((None, d, d), lambda l, i: (l, 0, i)),
            pl.BlockSpec((None, 1, d), lambda l, i: (l, 0, i)),
        ],
        out_specs=pl.BlockSpec((None, None, 8, d), lambda l, i: (l, i, 0, 0)),
        out_shape=jax.ShapeDtypeStruct((depth, n_mod, 8, d), F32),
        compiler_params=_params(("parallel", "parallel")),
        name="adaln",
    )(cond, w_ada, b_ada.reshape(depth, 1, n))


_IN_SPLITS = (0, 3 * D_MODEL, 3 * D_MODEL + 512, 3 * D_MODEL + 1024, 3 * D_MODEL + 1536,
              3 * D_MODEL + 1536 + RWKV_IN, 3 * D_MODEL + 1536 + RWKV_IN + 2 * BRANCH_WIDTH)
_IN_COL_CHUNK = 512


def _inproj_kernel(x_ref, mod_ref, w_ref, gates_ref, q_ref, k_ref, v_ref, pr_ref, ps_ref):
    h = (x_ref[...] * (1.0 + mod_ref[1:2, :]) + mod_ref[0:1, :]).astype(BF16)
    outs = (gates_ref, q_ref, k_ref, v_ref, pr_ref, ps_ref)
    for idx, o_ref in enumerate(outs):
        lo, hi = _IN_SPLITS[idx], _IN_SPLITS[idx + 1]
        scale = HEAD_DIM ** -0.5 if idx == 1 else None
        for c0 in range(lo, hi, _IN_COL_CHUNK):
            c1 = min(c0 + _IN_COL_CHUNK, hi)
            r = jnp.dot(h, w_ref[:, c0:c1], preferred_element_type=F32)
            if scale is not None:
                r = r * scale
            o_ref[:, c0 - lo:c1 - lo] = r.astype(o_ref.dtype)


def _mod_row(n_ctx_tiles, batch):
    return lambda b, j: jnp.where(j < n_ctx_tiles, batch, b)


def _inproj(x, mods, w_in, layer, n_ctx_tiles):
    batch, t, d = x.shape
    nt = t // TOKEN_TILE
    d_in = w_in.shape[-1]
    row = _mod_row(n_ctx_tiles, batch)
    widths = [_IN_SPLITS[i + 1] - _IN_SPLITS[i] for i in range(6)]
    dtypes = [BF16, BF16, BF16, BF16, F32, BF16]
    return pl.pallas_call(
        _inproj_kernel,
        grid=(batch, nt),
        in_specs=[
            pl.BlockSpec((None, TOKEN_TILE, d), lambda b, j: (b, j, 0)),
            pl.BlockSpec((None, None, 6, d), lambda b, j: (layer, row(b, j), 0, 0)),
            pl.BlockSpec((None, d, d_in), lambda b, j: (layer, 0, 0), pipeline_mode=pl.Buffered(1)),
        ],
        out_specs=[pl.BlockSpec((None, TOKEN_TILE, w), lambda b, j: (b, j, 0)) for w in widths],
        out_shape=[jax.ShapeDtypeStruct((batch, t, w), dt) for w, dt in zip(widths, dtypes)],
        compiler_params=_params(("parallel", "parallel")),
        name="inproj",
    )(x, mods, w_in)


def _na_bias_table(rpb):
    cols = np.arange(GRID_W)
    col0 = np.clip(cols - NA_WIN_W // 2, 0, GRID_W - NA_WIN_W)
    kc = np.arange(GRID_W)
    in_win = (kc[None, :] >= col0[:, None]) & (kc[None, :] < col0[:, None] + NA_WIN_W)
    dcol = kc[None, :] - cols[:, None] + NA_WIN_W - 1
    n_dcol = 2 * NA_WIN_W - 1
    onehot = ((dcol[None] == np.arange(n_dcol)[:, None, None]) & in_win[None]).astype(np.float32)
    toe = jnp.einsum("hrd,dck->hrck", rpb, jnp.asarray(onehot), precision=HIGHEST)
    toe = toe + jnp.asarray(np.where(in_win, 0.0, MASK_VALUE).astype(np.float32))
    tab = jnp.stack([toe[:, NA_WIN_H - 1 - e:2 * NA_WIN_H - 1 - e] for e in range(NA_WIN_H)])
    tab = jnp.transpose(tab, (0, 1, 3, 2, 4))
    return tab.reshape(NA_WIN_H, NA_HEADS, GRID_W, NA_WIN_H * GRID_W)


NA_ROWS_PER_STEP = 4


def _na_kernel(q_ref, k_ref, v_ref, *rest, n_ctx, rows):
    bias_refs, o_ref = rest[:NA_ROWS_PER_STEP], rest[NA_ROWS_PER_STEP]
    j = pl.program_id(1)
    ctx_steps = n_ctx // (GRID_W * NA_ROWS_PER_STEP)
    lo2 = lax.broadcasted_iota(jnp.int32, (2 * GRID_W, LANES), 1) < HEAD_DIM
    top = lax.broadcasted_iota(jnp.int32, (2 * GRID_W, LANES), 0) < GRID_W
    own_head = lo2 == top
    lo = lax.broadcasted_iota(jnp.int32, (GRID_W, LANES), 1) < HEAD_DIM
    n_win = NA_WIN_H * GRID_W
    pairs = [slice(p * LANES, (p + 1) * LANES) for p in range(N_PAIRS)]

    def attend(i, local):
        qs = slice(i * GRID_W, (i + 1) * GRID_W)
        if local:
            r = (j - ctx_steps) * NA_ROWS_PER_STEP + i
            row0 = jnp.clip(r - NA_WIN_H // 2, 0, rows - NA_WIN_H)
            start = pl.multiple_of(n_ctx + row0 * GRID_W, GRID_W)
        q2 = []
        for cs in pairs:
            qp = q_ref[qs, cs].astype(F32)
            q2.append(jnp.where(own_head, jnp.concatenate([qp, qp], axis=0), 0.0).astype(BF16))
        s_c = [lax.dot_general(q_, k_ref[0:n_ctx, cs], _NT, preferred_element_type=F32)
               for q_, cs in zip(q2, pairs)]
        mx = [jnp.max(s, axis=-1, keepdims=True) for s in s_c]
        if local:
            s_l = [lax.dot_general(q_, k_ref[pl.ds(start, n_win), cs], _NT, preferred_element_type=F32)
                   + bias_refs[i][p] for p, (q_, cs) in enumerate(zip(q2, pairs))]
            mx = [jnp.maximum(m, jnp.max(s, axis=-1, keepdims=True)) for m, s in zip(mx, s_l)]
            p_l = [jnp.exp(s - m) for s, m in zip(s_l, mx)]
        p_c = [jnp.exp(s - m) for s, m in zip(s_c, mx)]
        den = [jnp.sum(x, axis=-1, keepdims=True) for x in p_c]
        acc = [jnp.dot(x.astype(BF16), v_ref[0:n_ctx, cs], preferred_element_type=F32)
               for x, cs in zip(p_c, pairs)]
        if local:
            den = [d_ + jnp.sum(x, axis=-1, keepdims=True) for d_, x in zip(den, p_l)]
            acc = [a + jnp.dot(x.astype(BF16), v_ref[pl.ds(start, n_win), cs], preferred_element_type=F32)
                   for a, x, cs in zip(acc, p_l, pairs)]
        for a, d_, cs in zip(acc, den, pairs):
            o = a / d_
            o_ref[qs, cs] = jnp.where(lo, o[:GRID_W], o[GRID_W:])

    @pl.when(j < ctx_steps)
    def _():
        for i in range(NA_ROWS_PER_STEP):
            attend(i, False)

    @pl.when(j >= ctx_steps)
    def _():
        for i in range(NA_ROWS_PER_STEP):
            attend(i, True)


def _na_attention(q, k, v, bias, n_ctx):
    batch, t, w = q.shape
    rows = (t - n_ctx) // GRID_W
    tq = GRID_W * NA_ROWS_PER_STEP
    assert n_ctx % tq == 0 and rows % NA_ROWS_PER_STEP == 0
    ctx_steps = n_ctx // tq
    bias = bias.reshape(NA_WIN_H, N_PAIRS, 2 * GRID_W, NA_WIN_H * GRID_W)

    def bias_spec(i):
        def idx(b, j):
            r = jnp.maximum(j - ctx_steps, 0) * NA_ROWS_PER_STEP + i
            row0 = jnp.clip(r - NA_WIN_H // 2, 0, rows - NA_WIN_H)
            return (r - row0, 0, 0, 0)
        return pl.BlockSpec((None, N_PAIRS, 2 * GRID_W, NA_WIN_H * GRID_W), idx)

    return pl.pallas_call(
        functools.partial(_na_kernel, n_ctx=n_ctx, rows=rows),
        grid=(batch, t // tq),
        in_specs=[
            pl.BlockSpec((None, tq, w), lambda b, j: (b, j, 0)),
            pl.BlockSpec((None, t, w), lambda b, j: (b, 0, 0)),
            pl.BlockSpec((None, t, w), lambda b, j: (b, 0, 0)),
        ] + [bias_spec(i) for i in range(NA_ROWS_PER_STEP)],
        out_specs=pl.BlockSpec((None, tq, w), lambda b, j: (b, j, 0)),
        out_shape=jax.ShapeDtypeStruct((batch, t, w), F32),
        compiler_params=_params(("parallel", "arbitrary")),
        name="na_attention",
    )(q, k, v, *([bias] * NA_ROWS_PER_STEP))


def _gelu_tanh(x):
    return x * (0.5 * (1.0 + jnp.tanh(0.7978845608028654 * (x + 0.044715 * (x * x * x)))))


_N_STAGE = 8


def _rwkv_prep_kernel(p_ref, prev_ref, next_ref, mup_ref, mun_ref, w0_ref, w2_ref, a0_ref, a2_ref, g2_ref,
                      kk_ref, ka_ref, rk_ref,
                      q1_ref, y0_ref, m_ref, n0_ref, g_ref, bv_ref, stage_ref, *, n_ctx_tiles, n_tiles):
    j = pl.program_id(1)
    tm = TOKEN_TILE
    w = BRANCH_WIDTH
    ch = RWKV_CHUNK

    p = p_ref[...]
    first = (j == 0) | (j == n_ctx_tiles)
    last = (j == n_ctx_tiles - 1) | (j == n_tiles - 1)
    prev_row = jnp.where(first, 0.0, prev_ref[7:8, :])
    next_row = jnp.where(last, 0.0, next_ref[0:1, :])
    row = lax.broadcasted_iota(jnp.int32, (tm, 1), 0)
    p_prev = jnp.where(row == 0, prev_row, pltpu.roll(p, 1, 0))
    p_next = jnp.where(row == tm - 1, next_row, pltpu.roll(p, tm - 1, 0))
    ps = p + mup_ref[...] * (p_prev - p) + mun_ref[...] * (p_next - p)

    r = ps[:, 0:w]
    k = ps[:, w:2 * w]
    v = ps[:, 2 * w:3 * w]
    wc = jnp.tanh(ps[:, 3 * w:3 * w + LANES])
    ac = ps[:, 3 * w + LANES:3 * w + 2 * LANES]
    gc = ps[:, 3 * w + 2 * LANES:3 * w + 3 * LANES]

    g_ref[...] = _dot(_sigmoid(gc), g2_ref[...])
    kk = k * kk_ref[...]
    kk = kk * lax.rsqrt(_head_sum(kk * kk) + 1e-12)

    lo_dir = lax.broadcasted_iota(jnp.int32, (tm, LANES), 1) < DECAY_LORA
    ti = lax.broadcasted_iota(jnp.int32, (tm, tm), 0)
    si = lax.broadcasted_iota(jnp.int32, (tm, tm), 1)
    same_chunk = (ti // ch) == (si // ch)
    ci = lax.broadcasted_iota(jnp.int32, (ch, QUAD), 0)
    cj = lax.broadcasted_iota(jnp.int32, (ch, QUAD), 1) % HEAD_DIM
    eye_w = jnp.where(ci == cj, 1.0, 0.0)
    head_of_lane = lax.broadcasted_iota(jnp.int32, (ch, QUAD), 1) // HEAD_DIM
    head_masks = [head_of_lane == h for h in range(QUAD // HEAD_DIM)]
    bi = lax.broadcasted_iota(jnp.int32, (QUAD, QUAD), 0) // HEAD_DIM
    bj = lax.broadcasted_iota(jnp.int32, (QUAD, QUAD), 1) // HEAD_DIM
    maskb = jnp.where(bi == bj, 1.0, 0.0).astype(BF16)
    zeros_w = jnp.zeros((ch, QUAD), F32)

    kd_sum = jnp.zeros((tm, w), F32)
    for d in range(2):
        md = lo_dir if d == 0 else ~lo_dir
        z = w0_ref[d:d + 1, :] + _dot(jnp.where(md, wc, 0.0), w2_ref[...])
        w_log = -(jnp.maximum(-z, 0.0) + jnp.log(1.0 + jnp.exp(-jnp.abs(z)))) - 0.5
        lw = -jnp.exp(w_log)
        a = _sigmoid(a0_ref[d:d + 1, :] + _dot(jnp.where(md, ac, 0.0), a2_ref[...]))
        kd = k * (1.0 + (a - 1.0) * ka_ref[...])
        kd_sum = kd_sum + kd
        beta = kk * a
        before = (si <= ti) if d == 0 else (si >= ti)
        tri = jnp.where(same_chunk & before, 1.0, 0.0)
        cum = _dot_hi(tri, lw)
        last_step = ch - 1 if d == 0 else 0
        tot = jnp.concatenate(
            [jnp.broadcast_to(cum[c * ch + last_step:c * ch + last_step + 1], (ch, w)) for c in range(tm // ch)],
            axis=0)
        e_neg = jnp.exp(-cum)
        e_hat = jnp.exp(tot - cum)
        stage_ref[0] = -kk * jnp.exp(cum - lw)
        stage_ref[1] = r * jnp.exp(cum)
        stage_ref[2] = beta * e_neg
        stage_ref[3] = kd * e_neg
        stage_ref[4] = beta * e_hat
        stage_ref[5] = kd * e_hat
        stage_ref[6] = v
        stage_ref[7] = jnp.exp(tot)
        strict = (cj < ci) if d == 0 else (cj > ci)
        incl = (cj <= ci) if d == 0 else (cj >= ci)

        def bd(x):
            return jnp.concatenate([x.astype(BF16)] * (QUAD // HEAD_DIM), axis=0) * maskb

        def wide_dot(lhs, rhs):
            return jnp.dot(lhs.astype(BF16), rhs, preferred_element_type=F32)

        def diag_blocks(full):
            out = zeros_w
            for h, hm in enumerate(head_masks):
                out = out + jnp.where(hm, full[h * HEAD_DIM:(h + 1) * HEAD_DIM], 0.0)
            return out

        chains = [(c, qd) for c in range(tm // ch) for qd in range(w // QUAD)]

        def load(i):
            return [stage_ref[i, c * ch:(c + 1) * ch, qd * QUAD:(qd + 1) * QUAD] for c, qd in chains]

        ab, rb, bt, kt = load(0), load(1), load(2), load(3)
        lhs = [jnp.concatenate([a_, r_], axis=0).astype(BF16) for a_, r_ in zip(ab, rb)]
        g1 = [lax.dot_general(l_, bd(x), _NT, preferred_element_type=F32) for l_, x in zip(lhs, bt)]
        g2 = [lax.dot_general(l_, bd(x), _NT, preferred_element_type=F32) for l_, x in zip(lhs, kt)]
        a_ab = [jnp.where(strict, g[:ch], 0.0) for g in g1]
        a_rb = [jnp.where(incl, g[ch:], 0.0) for g in g1]
        a_kk = [jnp.concatenate([jnp.where(strict, g[:ch], 0.0), jnp.where(incl, g[ch:], 0.0)], axis=0)
                for g in g2]
        tinv = [eye_w + a for a in a_ab]
        apow = [wide_dot(a, bd(a)) for a in a_ab]
        for _ in range(4):
            pt = [wide_dot(jnp.concatenate([p_, t_], axis=0), bd(p_)) for p_, t_ in zip(apow, tinv)]
            apow = [x[:ch] for x in pt]
            tinv = [t_ + x[ch:] for t_, x in zip(tinv, pt)]
        tinv = [t_ + wide_dot(t_, bd(p_)) for t_, p_ in zip(tinv, apow)]
        vv = load(6)
        zv = [wide_dot(a, bd(x)) for a, x in zip(a_kk, vv)]
        wm = [wide_dot(t_, jnp.concatenate([bd(a_), bd(z[:ch])], axis=1))
              for t_, a_, z in zip(tinv, ab, zv)]
        qy = [wide_dot(a, jnp.concatenate([bd(x[:, :QUAD]), bd(x[:, QUAD:])], axis=1))
              for a, x in zip(a_rb, wm)]
        bh, kh = load(4), load(5)
        for i, (c, qd) in enumerate(chains):
            rs = slice(c * ch, (c + 1) * ch)
            cs = slice(qd * QUAD, (qd + 1) * QUAD)
            q1_ref[d, rs, cs] = rb[i] + qy[i][:, :QUAD]
            y0_ref[d, rs, cs] = qy[i][:, QUAD:] + zv[i][ch:]
            lhs_t = jnp.concatenate([bh[i], kh[i]], axis=0).astype(BF16)
            rhs = jnp.concatenate([wm[i], jnp.concatenate([zeros_w, vv[i]], axis=1)], axis=0).astype(BF16)
            mn = lax.dot_general(lhs_t, rhs, _TN, preferred_element_type=F32)
            dg = stage_ref[7, c * ch:c * ch + 1, cs]
            m_ref[d, c, :, cs] = diag_blocks(mn[:, :QUAD]) + jnp.where(eye_w > 0.0, dg, 0.0)
            n0_ref[d, c, :, cs] = diag_blocks(mn[:, QUAD:])

    bv_ref[...] = _head_sum(r * kd_sum * rk_ref[...]) * v


def _rwkv_prep(p_rwkv, prm, n_ctx_tiles):
    batch, t, wp = p_rwkv.shape
    nt = t // TOKEN_TILE
    cpt = TOKEN_TILE // RWKV_CHUNK
    nc = t // RWKV_CHUNK
    halo = TOKEN_TILE // 8
    w = BRANCH_WIDTH
    c2 = lambda b, j: (0, 0)
    vec = lambda n: pl.BlockSpec((1, n), c2)
    tok = pl.BlockSpec((None, TOKEN_TILE, w), lambda b, j: (b, j, 0))
    dtok = pl.BlockSpec((None, 2, TOKEN_TILE, w), lambda b, j: (b, 0, j, 0))
    mat = pl.BlockSpec((None, 2, cpt, HEAD_DIM, w), lambda b, j: (b, 0, j, 0, 0))
    return pl.pallas_call(
        functools.partial(_rwkv_prep_kernel, n_ctx_tiles=n_ctx_tiles, n_tiles=nt),
        grid=(batch, nt),
        in_specs=[
            pl.BlockSpec((None, TOKEN_TILE, wp), lambda b, j: (b, j, 0)),
            pl.BlockSpec((None, 8, wp), lambda b, j: (b, jnp.maximum(j * halo - 1, 0), 0)),
            pl.BlockSpec((None, 8, wp), lambda b, j: (b, jnp.minimum((j + 1) * halo, t // 8 - 1), 0)),
            vec(wp), vec(wp),
            pl.BlockSpec((2, w), c2), pl.BlockSpec((2 * DECAY_LORA, w), c2),
            pl.BlockSpec((2, w), c2), pl.BlockSpec((2 * DECAY_LORA, w), c2),
            pl.BlockSpec((GATE_LORA, w), c2),
            vec(w), vec(w), vec(w),
        ],
        out_specs=[dtok, dtok, mat, mat, tok, tok],
        out_shape=[
            jax.ShapeDtypeStruct((batch, 2, t, w), F32),
            jax.ShapeDtypeStruct((batch, 2, t, w), F32),
            jax.ShapeDtypeStruct((batch, 2, nc, HEAD_DIM, w), F32),
            jax.ShapeDtypeStruct((batch, 2, nc, HEAD_DIM, w), F32),
            jax.ShapeDtypeStruct((batch, t, w), F32),
            jax.ShapeDtypeStruct((batch, t, w), F32),
        ],
        scratch_shapes=[pltpu.VMEM((_N_STAGE, TOKEN_TILE, w), F32)],
        compiler_params=_params(("parallel", "parallel")),
        name="rwkv_prep",
    )(p_rwkv, p_rwkv, p_rwkv, prm["mu_prev"], prm["mu_next"], prm["w0"], prm["w2"], prm["a0"], prm["a2"],
      prm["g2"], prm["k_k"], prm["k_a"], prm["r_k"])


def _rwkv_chain_kernel(q1f_ref, y0f_ref, mf_ref, n0f_ref, q1b_ref, y0b_ref, mb_ref, n0b_ref,
                       yf_ref, yb_ref, s_ref):
    @pl.when(pl.program_id(0) == 0)
    def _():
        s_ref[...] = jnp.zeros_like(s_ref)

    batch = q1f_ref.shape[0]
    bi = lax.broadcasted_iota(jnp.int32, (QUAD, QUAD), 0) // HEAD_DIM
    bj = lax.broadcasted_iota(jnp.int32, (QUAD, QUAD), 1) // HEAD_DIM
    blk = bi == bj
    dirs = ((q1f_ref, y0f_ref, mf_ref, n0f_ref, yf_ref), (q1b_ref, y0b_ref, mb_ref, n0b_ref, yb_ref))
    chains = [(d, b, slice(qd * QUAD, (qd + 1) * QUAD))
              for d in range(2) for b in range(batch) for qd in range(BRANCH_WIDTH // QUAD)]
    outs = []
    for d, b, cs in chains:
        q1_ref, _, m_ref, _, _ = dirs[d]
        s = s_ref[d, b, :, cs]
        s_bd = jnp.where(blk, jnp.concatenate([s] * (QUAD // HEAD_DIM), axis=0), 0.0)
        outs.append(_dot_hi(jnp.concatenate([q1_ref[b, :, cs], m_ref[b, :, cs]], axis=0), s_bd))
    for (d, b, cs), o in zip(chains, outs):
        _, y0_ref, _, n0_ref, y_ref = dirs[d]
        y_ref[b, :, cs] = o[:RWKV_CHUNK] + y0_ref[b, :, cs]
        s_ref[d, b, :, cs] = o[RWKV_CHUNK:] + n0_ref[b, :, cs]


def _rwkv_chain(q1, y0, m, n0, n_ctx):
    batch, _, t, w = q1.shape
    nc = t // RWKV_CHUNK
    ncc = n_ctx // RWKV_CHUNK

    def chunk(d, s):
        return s if d == 0 else jnp.where(s < ncc, ncc - 1 - s, nc - 1 - (s - ncc))

    in_specs = []
    for d in range(2):
        tok = pl.BlockSpec((batch, None, RWKV_CHUNK, w), lambda s, d=d: (0, d, chunk(d, s), 0))
        mat = pl.BlockSpec((batch, None, None, HEAD_DIM, w), lambda s, d=d: (0, d, chunk(d, s), 0, 0))
        in_specs += [tok, tok, mat, mat]
    out_specs = [pl.BlockSpec((batch, RWKV_CHUNK, w), lambda s, d=d: (0, chunk(d, s), 0)) for d in range(2)]
    return pl.pallas_call(
        _rwkv_chain_kernel,
        grid=(nc,),
        in_specs=in_specs,
        out_specs=out_specs,
        out_shape=[jax.ShapeDtypeStruct((batch, t, w), F32)] * 2,
        scratch_shapes=[pltpu.VMEM((2, batch, HEAD_DIM, w), F32)],
        compiler_params=_params(("arbitrary",)),
        name="rwkv_chain",
    )(q1, y0, m, n0, q1, y0, m, n0)


_FF_CHUNK = 256


def _post_kernel(x_ref, mod_ref, gates_ref, ona_ref, yf_ref, yb_ref, g_ref, bv_ref, psgu_ref,
                 gng_ref, gnb_ref, slng_ref, slnb_ref, sw_ref, sb_ref,
                 wb_ref, wo_ref, ln1g_ref, ln1b_ref, wgu_ref, wd_ref, ln2g_ref, ln2b_ref,
                 o_ref, osgu_ref, act_ref, *, alpha):
    w = BRANCH_WIDTH
    d = D_MODEL

    gl = _gelu_tanh(psgu_ref[...].astype(F32))
    u = gl[:, :w]
    v = _layer_norm(gl[:, w:], slng_ref[...], slnb_ref[...], LN_EPS).astype(BF16)
    lo = lax.broadcasted_iota(jnp.int32, (SGU_CHUNK, LANES), 1) < HEAD_DIM
    for ck in range(TOKEN_TILE // SGU_CHUNK):
        rs = slice(ck * SGU_CHUNK, (ck + 1) * SGU_CHUNK)
        for p in range(N_PAIRS):
            cs = slice(p * LANES, (p + 1) * LANES)
            vp = v[rs, cs]
            r0 = jnp.dot(sw_ref[2 * p], vp, preferred_element_type=F32)
            r1 = jnp.dot(sw_ref[2 * p + 1], vp, preferred_element_type=F32)
            osgu_ref[rs, cs] = (u[rs, cs] * (jnp.where(lo, r0, r1) + sb_ref[:, cs])).astype(BF16)

    y = yf_ref[...] + yb_ref[...]
    mu = _head_sum(y) * (1.0 / HEAD_DIM)
    yc = y - mu
    var = _head_sum(yc * yc) * (1.0 / HEAD_DIM)
    y = yc * lax.rsqrt(var + RWKV_GN_EPS) * gng_ref[...] + gnb_ref[...]
    o_rwkv = (y + bv_ref[...]) * g_ref[...]

    acc = _sigmoid(gates_ref[:, 0:d].astype(F32)) * _dot(ona_ref[...], wb_ref[0])
    acc = acc + _sigmoid(gates_ref[:, d:2 * d].astype(F32)) * _dot(o_rwkv, wb_ref[1])
    acc = acc + _sigmoid(gates_ref[:, 2 * d:3 * d].astype(F32)) * jnp.dot(
        osgu_ref[...], wb_ref[2], preferred_element_type=F32)
    z = _dot(acc, wo_ref[...])
    x1 = _layer_norm(alpha * x_ref[...] + mod_ref[2:3, :] * z, ln1g_ref[...], ln1b_ref[...], LN_EPS)

    h = (x1 * (1.0 + mod_ref[4:5, :]) + mod_ref[3:4, :]).astype(BF16)
    for c0 in range(0, D_FF, _FF_CHUNK):
        gate = jnp.dot(h, wgu_ref[:, c0:c0 + _FF_CHUNK], preferred_element_type=F32)
        up = jnp.dot(h, wgu_ref[:, D_FF + c0:D_FF + c0 + _FF_CHUNK], preferred_element_type=F32)
        act_ref[:, c0:c0 + _FF_CHUNK] = (gate * _sigmoid(gate) * up).astype(BF16)
    z = jnp.dot(act_ref[...], wd_ref[...], preferred_element_type=F32)
    o_ref[...] = _layer_norm(alpha * x1 + mod_ref[5:6, :] * z, ln2g_ref[...], ln2b_ref[...], LN_EPS)


def _post(x, mods, gates, o_na, y_f, y_b, g, bv, p_sgu, vecs, sgu_w, sgu_b_full, w_branch, w_out, w_gu, w_down,
          layer, n_ctx_tiles, alpha, skip_tiles):
    batch, t, d = x.shape
    nt = t // TOKEN_TILE - skip_tiles
    w = BRANCH_WIDTH
    row = _mod_row(n_ctx_tiles, batch)
    c2 = lambda b, j: (0, 0)
    tok = lambda n: pl.BlockSpec((None, TOKEN_TILE, n), lambda b, j: (b, j + skip_tiles, 0))
    vw = pl.BlockSpec((1, w), c2)
    vd = pl.BlockSpec((1, d), c2)
    once = dict(pipeline_mode=pl.Buffered(1))
    return pl.pallas_call(
        functools.partial(_post_kernel, alpha=alpha),
        grid=(batch, nt),
        in_specs=[
            tok(d),
            pl.BlockSpec((None, None, 6, d), lambda b, j: (layer, row(b, j + skip_tiles), 0, 0)),
            tok(3 * d), tok(w), tok(w), tok(w), tok(w), tok(w), tok(2 * w),
            vw, vw, vw, vw,
            pl.BlockSpec((None, SGU_GROUPS, SGU_CHUNK, SGU_CHUNK), lambda b, j: (layer, 0, 0, 0)),
            pl.BlockSpec((SGU_CHUNK, w), c2),
            pl.BlockSpec((None, 3, w, d), lambda b, j: (layer, 0, 0, 0), **once),
            pl.BlockSpec((None, d, d), lambda b, j: (layer, 0, 0), **once),
            vd, vd,
            pl.BlockSpec((None, d, 2 * D_FF), lambda b, j: (layer, 0, 0), **once),
            pl.BlockSpec((None, D_FF, d), lambda b, j: (layer, 0, 0), **once),
            vd, vd,
        ],
        out_specs=pl.BlockSpec((None, TOKEN_TILE, d), lambda b, j: (b, j, 0)),
        out_shape=jax.ShapeDtypeStruct((batch, nt * TOKEN_TILE, d), F32),
        scratch_shapes=[pltpu.VMEM((TOKEN_TILE, w), BF16), pltpu.VMEM((TOKEN_TILE, D_FF), BF16)],
        compiler_params=_params(("parallel", "parallel")),
        name="post",
    )(x, mods, gates, o_na, y_f, y_b, g, bv, p_sgu, vecs["gn_g"], vecs["gn_b"], vecs["sgu_ln_g"],
      vecs["sgu_ln_b"], sgu_w, sgu_b_full, w_branch, w_out, vecs["ln1_g"], vecs["ln1_b"], w_gu, w_down,
      vecs["ln2_g"], vecs["ln2_b"])


def kernel(x, c, ctx, c_ctx, w_ada, b_ada, w_in, na_rpb, rwkv_mu_prev, rwkv_mu_next, rwkv_w0, rwkv_w2, rwkv_a0,
           rwkv_a2, rwkv_g2, rwkv_k_k, rwkv_k_a, rwkv_r_k, rwkv_gn_g, rwkv_gn_b, sgu_ln_g, sgu_ln_b, sgu_w, sgu_b,
           w_branch, w_out, ln1_g, ln1_b, ln2_g, ln2_b, ffn_w_gu, ffn_w_down):
    batch, seq, d = x.shape
    n_ctx = ctx.shape[1]
    depth = w_ada.shape[0]
    assert d == D_MODEL and batch <= 7
    assert n_ctx % TOKEN_TILE == 0 and seq % TOKEN_TILE == 0
    assert seq // GRID_W >= NA_WIN_H
    n_ctx_tiles = n_ctx // TOKEN_TILE
    alpha = float((2 * depth) ** 0.25)

    cond = jnp.zeros((8, d), F32).at[:batch].set(c).at[batch].set(c_ctx)
    mods = jnp.transpose(_ada_all(cond, w_ada, b_ada), (0, 2, 1, 3))

    xs = jnp.concatenate([ctx, x], axis=1)
    w_in_b = w_in.astype(BF16)
    w_branch_b = w_branch.astype(BF16)
    w_out_b = w_out.astype(BF16)
    w_gu_b = ffn_w_gu.astype(BF16)
    w_down_b = ffn_w_down.astype(BF16)
    sgu_w_b = sgu_w.astype(BF16)
    row = lambda a: a.reshape(1, -1)

    for l in range(depth):
        last = l == depth - 1
        gates, q, k, v, p_rwkv, p_sgu = _inproj(xs, mods, w_in_b, l, n_ctx_tiles)
        o_na = _na_attention(q, k, v, _na_bias_table(na_rpb[l]), n_ctx)
        prm = dict(
            mu_prev=row(rwkv_mu_prev[l]), mu_next=row(rwkv_mu_next[l]),
            w0=rwkv_w0[l], w2=rwkv_w2[l].reshape(2 * DECAY_LORA, BRANCH_WIDTH).astype(BF16),
            a0=rwkv_a0[l], a2=rwkv_a2[l].reshape(2 * DECAY_LORA, BRANCH_WIDTH).astype(BF16),
            g2=rwkv_g2[l].astype(BF16), k_k=row(rwkv_k_k[l]), k_a=row(rwkv_k_a[l]), r_k=row(rwkv_r_k[l]))
        q1, y0, m, n0, g, bv = _rwkv_prep(p_rwkv, prm, n_ctx_tiles)
        y_f, y_b = _rwkv_chain(q1, y0, m, n0, n_ctx)
        b_full = jnp.repeat(jnp.transpose(sgu_b[l]), BRANCH_WIDTH // SGU_GROUPS, axis=1)
        vecs = dict(gn_g=row(rwkv_gn_g[l]), gn_b=row(rwkv_gn_b[l]), sgu_ln_g=row(sgu_ln_g[l]),
                    sgu_ln_b=row(sgu_ln_b[l]), ln1_g=row(ln1_g[l]), ln1_b=row(ln1_b[l]), ln2_g=row(ln2_g[l]),
                    ln2_b=row(ln2_b[l]))
        xs = _post(xs, mods, gates, o_na, y_f, y_b, g, bv, p_sgu, vecs, sgu_w_b, b_full, w_branch_b, w_out_b,
                   w_gu_b, w_down_b, l, n_ctx_tiles, alpha, n_ctx_tiles if last else 0)
    return xs
```

```python
import functools

import jax
import jax.numpy as jnp
import numpy as np
from jax import lax
from jax.experimental import pallas as pl
from jax.experimental.pallas import tpu as pltpu

F32 = jnp.float32
BF16 = jnp.bfloat16

D_MODEL = 1024
GRID_W = 64
NA_HEADS = 8
HEAD_DIM = 64
NA_WIN_H = 8
NA_WIN_W = 16
BRANCH_WIDTH = 512
DECAY_LORA = 64
GATE_LORA = 128
RWKV_IN = 3 * BRANCH_WIDTH + 4 * DECAY_LORA + GATE_LORA
RWKV_GN_EPS = 64e-5
SGU_CHUNK = 128
SGU_GROUPS = 8
D_FF = 2816
LN_EPS = 1e-5
MASK_VALUE = -1e30

LANES = 128
TOKEN_TILE = 256
RWKV_CHUNK = 64
N_PAIRS = NA_HEADS // 2
QUAD = 4 * HEAD_DIM
HIGHEST = lax.Precision.HIGHEST
VMEM_LIMIT = 56 * 1024 * 1024

_NT = (((1,), (1,)), ((), ()))


def _params(sem, vmem=VMEM_LIMIT):
    return pltpu.CompilerParams(dimension_semantics=sem, vmem_limit_bytes=vmem)


def _dot(a, b):
    return jnp.dot(a.astype(BF16), b.astype(BF16), preferred_element_type=F32)


def _layer_norm(x, g, b, eps):
    mu = jnp.mean(x, axis=-1, keepdims=True)
    xc = x - mu
    var = jnp.mean(xc * xc, axis=-1, keepdims=True)
    return xc * lax.rsqrt(var + eps) * g + b


def _sigmoid(x):
    return 1.0 / (1.0 + jnp.exp(-x))


def _head_sum(x):
    rows = x.shape[0]
    lo = lax.broadcasted_iota(jnp.int32, (rows, LANES), 1) < HEAD_DIM
    out = []
    for p in range(x.shape[1] // LANES):
        xp = x[:, p * LANES:(p + 1) * LANES]
        s_lo = jnp.sum(jnp.where(lo, xp, 0.0), axis=-1, keepdims=True)
        s_hi = jnp.sum(jnp.where(lo, 0.0, xp), axis=-1, keepdims=True)
        out.append(jnp.where(lo, s_lo, s_hi))
    return jnp.concatenate(out, axis=1)


def _ada_kernel(c_ref, w_ref, b_ref, o_ref):
    c = c_ref[...]
    o_ref[...] = _dot(c * _sigmoid(c), w_ref[...]) + b_ref[...]


def _ada_all(cond, w_ada, b_ada):
    depth, d, n = w_ada.shape
    n_mod = n // d
    return pl.pallas_call(
        _ada_kernel,
        grid=(depth, n_mod),
        in_specs=[
            pl.BlockSpec((8, d), lambda l, i: (0, 0)),
            pl.BlockSpec((None, d, d), lambda l, i: (l, 0, i)),
            pl.BlockSpec((None, 1, d), lambda l, i: (l, 0, i)),
        ],
        out_specs=pl.BlockSpec((None, None, 8, d), lambda l, i: (l, i, 0, 0)),
        out_shape=jax.ShapeDtypeStruct((depth, n_mod, 8, d), F32),
        compiler_params=_params(("parallel", "parallel")),
        name="adaln",
    )(cond, w_ada, b_ada.reshape(depth, 1, n))


_IN_SPLITS = (0, 3 * D_MODEL, 3 * D_MODEL + 512, 3 * D_MODEL + 1024, 3 * D_MODEL + 1536,
              3 * D_MODEL + 1536 + RWKV_IN, 3 * D_MODEL + 1536 + RWKV_IN + 2 * BRANCH_WIDTH)
_IN_COL_CHUNK = 512


def _inproj_kernel(x_ref, mod_ref, w_ref, gates_ref, q_ref, k_ref, v_ref, pr_ref, ps_ref):
    h = (x_ref[...] * (1.0 + mod_ref[1:2, :]) + mod_ref[0:1, :]).astype(BF16)
    outs = (gates_ref, q_ref, k_ref, v_ref, pr_ref, ps_ref)
    for idx, o_ref in enumerate(outs):
        lo, hi = _IN_SPLITS[idx], _IN_SPLITS[idx + 1]
        scale = HEAD_DIM ** -0.5 if idx == 1 else None
        for c0 in range(lo, hi, _IN_COL_CHUNK):
            c1 = min(c0 + _IN_COL_CHUNK, hi)
            r = jnp.dot(h, w_ref[:, c0:c1], preferred_element_type=F32)
            if scale is not None:
                r = r * scale
            o_ref[:, c0 - lo:c1 - lo] = r.astype(o_ref.dtype)


def _mod_row(n_ctx_tiles, batch):
    return lambda b, j: jnp.where(j < n_ctx_tiles, batch, b)


def _inproj(x, mods, w_in, layer, n_ctx_tiles):
    batch, t, d = x.shape
    nt = t // TOKEN_TILE
    d_in = w_in.shape[-1]
    row = _mod_row(n_ctx_tiles, batch)
    widths = [_IN_SPLITS[i + 1] - _IN_SPLITS[i] for i in range(6)]
    dtypes = [BF16, BF16, BF16, BF16, F32, BF16]
    return pl.pallas_call(
        _inproj_kernel,
        grid=(batch, nt),
        in_specs=[
            pl.BlockSpec((None, TOKEN_TILE, d), lambda b, j: (b, j, 0)),
            pl.BlockSpec((None, None, 6, d), lambda b, j: (layer, row(b, j), 0, 0)),
            pl.BlockSpec((None, d, d_in), lambda b, j: (layer, 0, 0), pipeline_mode=pl.Buffered(1)),
        ],
        out_specs=[pl.BlockSpec((None, TOKEN_TILE, w), lambda b, j: (b, j, 0)) for w in widths],
        out_shape=[jax.ShapeDtypeStruct((batch, t, w), dt) for w, dt in zip(widths, dtypes)],
        compiler_params=_params(("parallel", "parallel")),
        name="inproj",
    )(x, mods, w_in)


def _na_bias_table(rpb):
    cols = np.arange(GRID_W)
    col0 = np.clip(cols - NA_WIN_W // 2, 0, GRID_W - NA_WIN_W)
    kc = np.arange(GRID_W)
    in_win = (kc[None, :] >= col0[:, None]) & (kc[None, :] < col0[:, None] + NA_WIN_W)
    dcol = kc[None, :] - cols[:, None] + NA_WIN_W - 1
    n_dcol = 2 * NA_WIN_W - 1
    onehot = ((dcol[None] == np.arange(n_dcol)[:, None, None]) & in_win[None]).astype(np.float32)
    toe = jnp.einsum("hrd,dck->hrck", rpb, jnp.asarray(onehot), precision=HIGHEST)
    toe = toe + jnp.asarray(np.where(in_win, 0.0, MASK_VALUE).astype(np.float32))
    tab = jnp.stack([toe[:, NA_WIN_H - 1 - e:2 * NA_WIN_H - 1 - e] for e in range(NA_WIN_H)])
    tab = jnp.transpose(tab, (0, 1, 3, 2, 4))
    return tab.reshape(NA_WIN_H, NA_HEADS, GRID_W, NA_WIN_H * GRID_W)


NA_ROWS_PER_STEP = 4


def _na_kernel(q_ref, k_ref, v_ref, *rest, n_ctx, rows):
    bias_refs, o_ref = rest[:NA_ROWS_PER_STEP], rest[NA_ROWS_PER_STEP]
    j = pl.program_id(1)
    ctx_steps = n_ctx // (GRID_W * NA_ROWS_PER_STEP)
    lo2 = lax.broadcasted_iota(jnp.int32, (2 * GRID_W, LANES), 1) < HEAD_DIM
    top = lax.broadcasted_iota(jnp.int32, (2 * GRID_W, LANES), 0) < GRID_W
    own_head = lo2 == top
    lo = lax.broadcasted_iota(jnp.int32, (GRID_W, LANES), 1) < HEAD_DIM
    n_win = NA_WIN_H * GRID_W
    pairs = [slice(p * LANES, (p + 1) * LANES) for p in range(N_PAIRS)]

    def attend(i, local):
        qs = slice(i * GRID_W, (i + 1) * GRID_W)
        if local:
            r = (j - ctx_steps) * NA_ROWS_PER_STEP + i
            row0 = jnp.clip(r - NA_WIN_H // 2, 0, rows - NA_WIN_H)
            start = pl.multiple_of(n_ctx + row0 * GRID_W, GRID_W)
        q2 = []
        for cs in pairs:
            qp = q_ref[qs, cs].astype(F32)
            q2.append(jnp.where(own_head, jnp.concatenate([qp, qp], axis=0), 0.0).astype(BF16))
        s_c = [lax.dot_general(q_, k_ref[0:n_ctx, cs], _NT, preferred_element_type=F32)
               for q_, cs in zip(q2, pairs)]
        mx = [jnp.max(s, axis=-1, keepdims=True) for s in s_c]
        if local:
            s_l = [lax.dot_general(q_, k_ref[pl.ds(start, n_win), cs], _NT, preferred_element_type=F32)
                   + bias_refs[i][p] for p, (q_, cs) in enumerate(zip(q2, pairs))]
            mx = [jnp.maximum(m, jnp.max(s, axis=-1, keepdims=True)) for m, s in zip(mx, s_l)]
            p_l = [jnp.exp(s - m) for s, m in zip(s_l, mx)]
        p_c = [jnp.exp(s - m) for s, m in zip(s_c, mx)]
        den = [jnp.sum(x, axis=-1, keepdims=True) for x in p_c]
        acc = [jnp.dot(x.astype(BF16), v_ref[0:n_ctx, cs], preferred_element_type=F32)
               for x, cs in zip(p_c, pairs)]
        if local:
            den = [d_ + jnp.sum(x, axis=-1, keepdims=True) for d_, x in zip(den, p_l)]
            acc = [a + jnp.dot(x.astype(BF16), v_ref[pl.ds(start, n_win), cs], preferred_element_type=F32)
                   for a, x, cs in zip(acc, p_l, pairs)]
        for a, d_, cs in zip(acc, den, pairs):
            o = a / d_
            o_ref[qs, cs] = jnp.where(lo, o[:GRID_W], o[GRID_W:])

    @pl.when(j < ctx_steps)
    def _():
        for i in range(NA_ROWS_PER_STEP):
            attend(i, False)

    @pl.when(j >= ctx_steps)
    def _():
        for i in range(NA_ROWS_PER_STEP):
            attend(i, True)


def _na_attention(q, k, v, bias, n_ctx):
    batch, t, w = q.shape
    rows = (t - n_ctx) // GRID_W
    tq = GRID_W * NA_ROWS_PER_STEP
    assert n_ctx % tq == 0 and rows % NA_ROWS_PER_STEP == 0
    ctx_steps = n_ctx // tq
    bias = bias.reshape(NA_WIN_H, N_PAIRS, 2 * GRID_W, NA_WIN_H * GRID_W)

    def bias_spec(i):
        def idx(b, j):
            r = jnp.maximum(j - ctx_steps, 0) * NA_ROWS_PER_STEP + i
            row0 = jnp.clip(r - NA_WIN_H // 2, 0, rows - NA_WIN_H)
            return (r - row0, 0, 0, 0)
        return pl.BlockSpec((None, N_PAIRS, 2 * GRID_W, NA_WIN_H * GRID_W), idx)

    return pl.pallas_call(
        functools.partial(_na_kernel, n_ctx=n_ctx, rows=rows),
        grid=(batch, t // tq),
        in_specs=[
            pl.BlockSpec((None, tq, w), lambda b, j: (b, j, 0)),
            pl.BlockSpec((None, t, w), lambda b, j: (b, 0, 0)),
            pl.BlockSpec((None, t, w), lambda b, j: (b, 0, 0)),
        ] + [bias_spec(i) for i in range(NA_ROWS_PER_STEP)],
        out_specs=pl.BlockSpec((None, tq, w), lambda b, j: (b, j, 0)),
        out_shape=jax.ShapeDtypeStruct((batch, t, w), F32),
        compiler_params=_params(("parallel", "arbitrary")),
        name="na_attention",
    )(q, k, v, *([bias] * NA_ROWS_PER_STEP))


def _gelu_tanh(x):
    return x * (0.5 * (1.0 + jnp.tanh(0.7978845608028654 * (x + 0.044715 * (x * x * x)))))


_N_STAGE = 6


def _rwkv_prep_kernel(p_ref, prev_ref, next_ref, mup_ref, mun_ref, w0_ref, w2_ref, a0_ref, a2_ref, g2_ref,
                      kk_ref, ka_ref, rk_ref,
                      q1_ref, y0_ref, m_ref, n0_ref, g_ref, bv_ref, stage_ref, *, n_ctx_tiles, n_tiles):
    j = pl.program_id(1)
    tm = TOKEN_TILE
    w = BRANCH_WIDTH
    ch = RWKV_CHUNK

    p = p_ref[...]
    first = (j == 0) | (j == n_ctx_tiles)
    last = (j == n_ctx_tiles - 1) | (j == n_tiles - 1)
    prev_row = jnp.where(first, 0.0, prev_ref[7:8, :])
    next_row = jnp.where(last, 0.0, next_ref[0:1, :])
    row = lax.broadcasted_iota(jnp.int32, (tm, 1), 0)
    p_prev = jnp.where(row == 0, prev_row, pltpu.roll(p, 1, 0))
    p_next = jnp.where(row == tm - 1, next_row, pltpu.roll(p, tm - 1, 0))
    ps = p + mup_ref[...] * (p_prev - p) + mun_ref[...] * (p_next - p)

    r = ps[:, 0:w]
    k = ps[:, w:2 * w]
    v = ps[:, 2 * w:3 * w]
    wc = jnp.tanh(ps[:, 3 * w:3 * w + LANES])
    ac = ps[:, 3 * w + LANES:3 * w + 2 * LANES]
    gc = ps[:, 3 * w + 2 * LANES:3 * w + 3 * LANES]

    g_ref[...] = _dot(_sigmoid(gc), g2_ref[...])
    kk = k * kk_ref[...]
    kk = kk * lax.rsqrt(_head_sum(kk * kk) + 1e-12)

    lo_dir = lax.broadcasted_iota(jnp.int32, (tm, LANES), 1) < DECAY_LORA
    ti = lax.broadcasted_iota(jnp.int32, (tm, tm), 0)
    si = lax.broadcasted_iota(jnp.int32, (tm, tm), 1)
    same_chunk = (ti // ch) == (si // ch)
    ci = lax.broadcasted_iota(jnp.int32, (ch, QUAD), 0)
    cj = lax.broadcasted_iota(jnp.int32, (ch, QUAD), 1) % HEAD_DIM
    eye_w = jnp.where(ci == cj, 1.0, 0.0)
    bi =lax.broadcasted_iota(jnp.int32, (QUAD, QUAD), 0) // HEAD_DIM
    bj = lax.broadcasted_iota(jnp.int32, (QUAD, QUAD), 1) // HEAD_DIM
    maskb = jnp.where(bi == bj, 1.0, 0.0).astype(BF16)

    kd_sum = jnp.zeros((tm, w), F32)
    for d in range(2):
        md = lo_dir if d == 0 else ~lo_dir
        z = w0_ref[d:d + 1, :] + _dot(jnp.where(md, wc, 0.0), w2_ref[...])
        lw = -float(np.exp(-0.5)) * _sigmoid(z)
        a = _sigmoid(a0_ref[d:d + 1, :] + _dot(jnp.where(md, ac, 0.0), a2_ref[...]))
        kd = k * (1.0 + (a - 1.0) * ka_ref[...])
        kd_sum = kd_sum + kd
        beta = kk * a
        before = (si <= ti) if d == 0 else (si >= ti)
        tri = jnp.where(same_chunk & before, 1.0, 0.0).astype(BF16)
        lw_hi = lw.astype(BF16)
        rem = lw - lw_hi.astype(F32)
        lw_mid = rem.astype(BF16)
        lw_lo = (rem - lw_mid.astype(F32)).astype(BF16)
        parts = jnp.dot(tri, jnp.concatenate([lw_hi, lw_mid, lw_lo], axis=1), preferred_element_type=F32)
        cum = (parts[:, 2 * w:] + parts[:, w:2 * w]) + parts[:, :w]
        last_step = ch - 1 if d == 0 else 0
        tot = jnp.concatenate(
            [jnp.broadcast_to(cum[c * ch + last_step:c * ch + last_step + 1], (ch, w)) for c in range(tm // ch)],
            axis=0)
        e_neg = jnp.exp(-cum)
        stage_ref[0] = -kk * jnp.exp(cum - lw)
        stage_ref[1] = r * jnp.exp(cum)
        stage_ref[2] = beta * e_neg
        stage_ref[3] = kd * e_neg
        stage_ref[4] = v
        stage_ref[5] = jnp.exp(tot)
        strict = (cj < ci) if d == 0 else (cj > ci)
        incl = (cj <= ci) if d == 0 else (cj >= ci)

        def bd(x):
            return jnp.concatenate([x.astype(BF16)] * (QUAD // HEAD_DIM), axis=0) * maskb

        def wide_dot(lhs, rhs):
            return jnp.dot(lhs.astype(BF16), rhs, preferred_element_type=F32)

        chains = [(c, qd) for c in range(tm // ch) for qd in range(w // QUAD)]
        n_ch = len(chains)

        def load(i):
            return [stage_ref[i, c * ch:(c + 1) * ch, qd * QUAD:(qd + 1) * QUAD] for c, qd in chains]

        ab, rb, bt, kt = load(0), load(1), load(2), load(3)
        lhs = [jnp.concatenate([a_, r_, eye_w], axis=0).astype(BF16) for a_, r_ in zip(ab, rb)]
        g1 = [lax.dot_general(l_, bd(x), _NT, preferred_element_type=F32) for l_, x in zip(lhs, bt)]
        g2 = [lax.dot_general(l_, bd(x), _NT, preferred_element_type=F32) for l_, x in zip(lhs, kt)]
        diag_w = [jnp.where(eye_w > 0.0, stage_ref[5, c * ch:c * ch + 1, qd * QUAD:(qd + 1) * QUAD], 0.0)
                  for c, qd in chains]
        scale = wide_dot(jnp.concatenate(diag_w, axis=0), maskb)
        scale = [scale[i * ch:(i + 1) * ch] for i in range(n_ch)]
        apow = [jnp.where(strict, g[:ch], 0.0) for g in g1]
        h_rb = [jnp.where(incl, g[ch:2 * ch], 0.0) for g in g1]
        j_bh = [s_ * g[2 * ch:] for s_, g in zip(scale, g1)]
        a_ak = [jnp.where(strict, g[:ch], 0.0) for g in g2]
        tail = [jnp.concatenate([jnp.where(incl, g[ch:2 * ch], 0.0), s_ * g[2 * ch:]], axis=0)
                for s_, g in zip(scale, g2)]
        for step in range(6):
            grow = step < 5
            stacked = [jnp.concatenate(([p_] if grow else []) + [h_, j_], axis=0)
                       for p_, h_, j_ in zip(apow, h_rb, j_bh)]
            out = [wide_dot(l_, bd(p_)) for l_, p_ in zip(stacked, apow)]
            off = ch if grow else 0
            if grow:
                apow = [o[:ch] for o in out]
            h_rb = [x + o[off:off + ch] for x, o in zip(h_rb, out)]
            j_bh = [x + o[off + ch:] for x, o in zip(j_bh, out)]
        hj = [jnp.concatenate([h_, j_], axis=0).astype(BF16) for h_, j_ in zip(h_rb, j_bh)]
        x12 = [jnp.dot(l_, jnp.concatenate([bd(a_), bd(k_)], axis=1), preferred_element_type=F32)
               for l_, a_, k_ in zip(hj, ab, a_ak)]
        vv = load(4)
        x3 = [wide_dot(x[:, QUAD:] + t_, bd(v_)) for x, t_, v_ in zip(x12, tail, vv)]
        for i, (c, qd) in enumerate(chains):
            rs = slice(c * ch, (c + 1) * ch)
            cs = slice(qd * QUAD, (qd + 1) * QUAD)
            q1_ref[d, rs, cs] = rb[i] + x12[i][:ch, :QUAD]
            y0_ref[d, rs, cs] = x3[i][:ch]
            m_ref[d, c, :, cs] = x12[i][ch:, :QUAD] + diag_w[i]
            n0_ref[d, c, :, cs] = x3[i][ch:]

    bv_ref[...] = _head_sum(r * kd_sum * rk_ref[...]) * v


def _rwkv_prep(p_rwkv, prm, n_ctx_tiles):
    batch, t, wp = p_rwkv.shape
    nt = t // TOKEN_TILE
    cpt = TOKEN_TILE // RWKV_CHUNK
    nc = t // RWKV_CHUNK
    halo = TOKEN_TILE // 8
    w = BRANCH_WIDTH
    c2 = lambda b, j: (0, 0)
    vec = lambda n: pl.BlockSpec((1, n), c2)
    tok = pl.BlockSpec((None, TOKEN_TILE, w), lambda b, j: (b, j, 0))
    dtok = pl.BlockSpec((None, 2, TOKEN_TILE, w), lambda b, j: (b, 0, j, 0))
    mat = pl.BlockSpec((None, 2, cpt, HEAD_DIM, w), lambda b, j: (b, 0, j, 0, 0))
    return pl.pallas_call(
        functools.partial(_rwkv_prep_kernel, n_ctx_tiles=n_ctx_tiles, n_tiles=nt),
        grid=(batch, nt),
        in_specs=[
            pl.BlockSpec((None, TOKEN_TILE, wp), lambda b, j: (b, j, 0)),
            pl.BlockSpec((None, 8, wp), lambda b, j: (b, jnp.maximum(j * halo - 1, 0), 0)),
            pl.BlockSpec((None, 8, wp), lambda b, j: (b, jnp.minimum((j + 1) * halo, t // 8 - 1), 0)),
            vec(wp), vec(wp),
            pl.BlockSpec((2, w), c2), pl.BlockSpec((2 * DECAY_LORA, w), c2),
            pl.BlockSpec((2, w), c2), pl.BlockSpec((2 * DECAY_LORA, w), c2),
            pl.BlockSpec((GATE_LORA, w), c2),
            vec(w), vec(w), vec(w),
        ],
        out_specs=[dtok, dtok, mat, mat, tok, tok],
        out_shape=[
            jax.ShapeDtypeStruct((batch, 2, t, w), F32),
            jax.ShapeDtypeStruct((batch, 2, t, w), F32),
            jax.ShapeDtypeStruct((batch, 2, nc, HEAD_DIM, w), F32),
            jax.ShapeDtypeStruct((batch, 2, nc, HEAD_DIM, w), F32),
            jax.ShapeDtypeStruct((batch, t, w), F32),
            jax.ShapeDtypeStruct((batch, t, w), F32),
        ],
        scratch_shapes=[pltpu.VMEM((_N_STAGE, TOKEN_TILE, w), F32)],
        compiler_params=_params(("parallel", "parallel")),
        name="rwkv_prep",
    )(p_rwkv, p_rwkv, p_rwkv, prm["mu_prev"], prm["mu_next"], prm["w0"], prm["w2"], prm["a0"], prm["a2"],
      prm["g2"], prm["k_k"], prm["k_a"], prm["r_k"])


def _rwkv_chain_kernel(q1f_ref, y0f_ref, mf_ref, n0f_ref, q1b_ref, y0b_ref, mb_ref, n0b_ref,
                       yf_ref, yb_ref, s_ref):
    @pl.when(pl.program_id(0) == 0)
    def _():
        s_ref[...] = jnp.zeros_like(s_ref)

    batch = q1f_ref.shape[0]
    ch = RWKV_CHUNK
    bi = lax.broadcasted_iota(jnp.int32, (QUAD, QUAD), 0) // HEAD_DIM
    bj = lax.broadcasted_iota(jnp.int32, (QUAD, QUAD), 1) // HEAD_DIM
    maskb = jnp.where(bi == bj, 1.0, 0.0).astype(BF16)
    dirs = ((q1f_ref, y0f_ref, mf_ref, n0f_ref, yf_ref), (q1b_ref, y0b_ref, mb_ref, n0b_ref, yb_ref))

    def split(x):
        hi = x.astype(BF16)
        return hi, (x - hi.astype(F32)).astype(BF16)

    def bd(xb):
        return jnp.concatenate([xb] * (QUAD // HEAD_DIM), axis=0) * maskb

    chains = [(d, b, slice(qd * QUAD, (qd + 1) * QUAD))
              for d in range(2) for b in range(batch) for qd in range(BRANCH_WIDTH // QUAD)]
    outs = []
    for d, b, cs in chains:
        q1_ref, _, m_ref, _, _ = dirs[d]
        s_hi, s_lo = split(s_ref[d, b, :, cs])
        m_hi, m_lo = split(m_ref[b, :, cs])
        lhs = jnp.concatenate([q1_ref[b, :, cs].astype(BF16), m_hi, m_lo], axis=0)
        o = jnp.dot(lhs, bd(s_hi), preferred_element_type=F32)
        o_lo = jnp.dot(m_hi, bd(s_lo), preferred_element_type=F32)
        outs.append((o[:ch], o[ch:2 * ch] + o[2 * ch:] + o_lo))
    for (d, b, cs), (y, s_new) in zip(chains, outs):
        _, y0_ref, _, n0_ref, y_ref = dirs[d]
        y_ref[b, :, cs] = y + y0_ref[b, :, cs]
        s_ref[d, b, :, cs] = s_new + n0_ref[b, :, cs]


def _rwkv_chain(q1, y0, m, n0, n_ctx):
    batch, _, t, w = q1.shape
    nc = t // RWKV_CHUNK
    ncc = n_ctx // RWKV_CHUNK

    def chunk(d, s):
        return s if d == 0 else jnp.where(s < ncc, ncc - 1 - s, nc - 1 - (s - ncc))

    in_specs = []
    for d in range(2):
        tok = pl.BlockSpec((batch, None, RWKV_CHUNK, w), lambda s, d=d: (0, d, chunk(d, s), 0))
        mat = pl.BlockSpec((batch, None, None, HEAD_DIM, w), lambda s, d=d: (0, d, chunk(d, s), 0, 0))
        in_specs += [tok, tok, mat, mat]
    out_specs = [pl.BlockSpec((batch, RWKV_CHUNK, w), lambda s, d=d: (0, chunk(d, s), 0)) for d in range(2)]
    return pl.pallas_call(
        _rwkv_chain_kernel,
        grid=(nc,),
        in_specs=in_specs,
        out_specs=out_specs,
        out_shape=[jax.ShapeDtypeStruct((batch, t, w), F32)] * 2,
        scratch_shapes=[pltpu.VMEM((2, batch, HEAD_DIM, w), F32)],
        compiler_params=_params(("arbitrary",)),
        name="rwkv_chain",
    )(q1, y0, m, n0, q1, y0, m, n0)


_FF_CHUNK = 256


def _post_kernel(x_ref, mod_ref, gates_ref, ona_ref, yf_ref, yb_ref, g_ref, bv_ref, psgu_ref,
                 gng_ref, gnb_ref, slng_ref, slnb_ref, sw_ref, sb_ref,
                 wb_ref, wo_ref, ln1g_ref, ln1b_ref, wgu_ref, wd_ref, ln2g_ref, ln2b_ref,
                 o_ref, osgu_ref, act_ref, *, alpha):
    w = BRANCH_WIDTH
    d = D_MODEL

    gl = _gelu_tanh(psgu_ref[...].astype(F32))
    u = gl[:, :w]
    v = _layer_norm(gl[:, w:], slng_ref[...], slnb_ref[...], LN_EPS).astype(BF16)
    lo = lax.broadcasted_iota(jnp.int32, (SGU_CHUNK, LANES), 1) < HEAD_DIM
    for ck in range(TOKEN_TILE // SGU_CHUNK):
        rs = slice(ck * SGU_CHUNK, (ck + 1) * SGU_CHUNK)
        for p in range(N_PAIRS):
            cs = slice(p * LANES, (p + 1) * LANES)
            vp = v[rs, cs]
            r0 = jnp.dot(sw_ref[2 * p], vp, preferred_element_type=F32)
            r1 = jnp.dot(sw_ref[2 * p + 1], vp, preferred_element_type=F32)
            osgu_ref[rs, cs] = (u[rs, cs] * (jnp.where(lo, r0, r1) + sb_ref[:, cs])).astype(BF16)

    y = yf_ref[...] + yb_ref[...]
    mu = _head_sum(y) * (1.0 / HEAD_DIM)
    yc = y - mu
    var = _head_sum(yc * yc) * (1.0 / HEAD_DIM)
    y = yc * lax.rsqrt(var + RWKV_GN_EPS) * gng_ref[...] + gnb_ref[...]
    o_rwkv = (y + bv_ref[...]) * g_ref[...]

    acc = _sigmoid(gates_ref[:, 0:d].astype(F32)) * _dot(ona_ref[...], wb_ref[0])
    acc = acc + _sigmoid(gates_ref[:, d:2 * d].astype(F32)) * _dot(o_rwkv, wb_ref[1])
    acc = acc + _sigmoid(gates_ref[:, 2 * d:3 * d].astype(F32)) * jnp.dot(
        osgu_ref[...], wb_ref[2], preferred_element_type=F32)
    z = _dot(acc, wo_ref[...])
    x1 = _layer_norm(alpha * x_ref[...] + mod_ref[2:3, :] * z, ln1g_ref[...], ln1b_ref[...], LN_EPS)

    h = (x1 * (1.0 + mod_ref[4:5, :]) + mod_ref[3:4, :]).astype(BF16)
    for c0 in range(0, D_FF, _FF_CHUNK):
        gate = jnp.dot(h, wgu_ref[:, c0:c0 + _FF_CHUNK], preferred_element_type=F32)
        up = jnp.dot(h, wgu_ref[:, D_FF + c0:D_FF + c0 + _FF_CHUNK], preferred_element_type=F32)
        act_ref[:, c0:c0 + _FF_CHUNK] = (gate * _sigmoid(gate) * up).astype(BF16)
    z = jnp.dot(act_ref[...], wd_ref[...], preferred_element_type=F32)
    o_ref[...] = _layer_norm(alpha * x1 + mod_ref[5:6, :] * z, ln2g_ref[...], ln2b_ref[...], LN_EPS)


def _post(x, mods, gates, o_na, y_f, y_b, g, bv, p_sgu, vecs, sgu_w, sgu_b_full, w_branch, w_out, w_gu, w_down,
          layer, n_ctx_tiles, alpha, skip_tiles):
    batch, t, d = x.shape
    nt = t // TOKEN_TILE - skip_tiles
    w = BRANCH_WIDTH
    row = _mod_row(n_ctx_tiles, batch)
    c2 = lambda b, j: (0, 0)
    tok = lambda n: pl.BlockSpec((None, TOKEN_TILE, n), lambda b, j: (b, j + skip_tiles, 0))
    vw = pl.BlockSpec((1, w), c2)
    vd = pl.BlockSpec((1, d), c2)
    once = dict(pipeline_mode=pl.Buffered(1))
    return pl.pallas_call(
        functools.partial(_post_kernel, alpha=alpha),
        grid=(batch, nt),
        in_specs=[
            tok(d),
            pl.BlockSpec((None, None, 6, d), lambda b, j: (layer, row(b, j + skip_tiles), 0, 0)),
            tok(3 * d), tok(w), tok(w), tok(w), tok(w), tok(w), tok(2 * w),
            vw, vw, vw, vw,
            pl.BlockSpec((None, SGU_GROUPS, SGU_CHUNK, SGU_CHUNK), lambda b, j: (layer, 0, 0, 0)),
            pl.BlockSpec((SGU_CHUNK, w), c2),
            pl.BlockSpec((None, 3, w, d), lambda b, j: (layer, 0, 0, 0), **once),
            pl.BlockSpec((None, d, d), lambda b, j: (layer, 0, 0), **once),
            vd, vd,
            pl.BlockSpec((None, d, 2 * D_FF), lambda b, j: (layer, 0, 0), **once),
            pl.BlockSpec((None, D_FF, d), lambda b, j: (layer, 0, 0), **once),
            vd, vd,
        ],
        out_specs=pl.BlockSpec((None, TOKEN_TILE, d), lambda b, j: (b, j, 0)),
        out_shape=jax.ShapeDtypeStruct((batch, nt * TOKEN_TILE, d), F32),
        scratch_shapes=[pltpu.VMEM((TOKEN_TILE, w), BF16), pltpu.VMEM((TOKEN_TILE, D_FF), BF16)],
        compiler_params=_params(("parallel", "parallel")),
        name="post",
    )(x, mods, gates, o_na, y_f, y_b, g, bv, p_sgu, vecs["gn_g"], vecs["gn_b"], vecs["sgu_ln_g"],
      vecs["sgu_ln_b"], sgu_w, sgu_b_full, w_branch, w_out, vecs["ln1_g"], vecs["ln1_b"], w_gu, w_down,
      vecs["ln2_g"], vecs["ln2_b"])


def kernel(x, c, ctx, c_ctx, w_ada, b_ada, w_in, na_rpb, rwkv_mu_prev, rwkv_mu_next, rwkv_w0, rwkv_w2, rwkv_a0,
           rwkv_a2, rwkv_g2, rwkv_k_k, rwkv_k_a, rwkv_r_k, rwkv_gn_g, rwkv_gn_b, sgu_ln_g, sgu_ln_b, sgu_w, sgu_b,
           w_branch, w_out, ln1_g, ln1_b, ln2_g, ln2_b, ffn_w_gu, ffn_w_down):
    batch, seq, d = x.shape
    n_ctx = ctx.shape[1]
    depth = w_ada.shape[0]
    assert d == D_MODEL and batch <= 7
    assert n_ctx % TOKEN_TILE == 0 and seq % TOKEN_TILE == 0
    assert seq // GRID_W >= NA_WIN_H
    n_ctx_tiles = n_ctx // TOKEN_TILE
    alpha = float((2 * depth) ** 0.25)

    cond = jnp.zeros((8, d), F32).at[:batch].set(c).at[batch].set(c_ctx)
    mods = jnp.transpose(_ada_all(cond, w_ada, b_ada), (0, 2, 1, 3))

    xs = jnp.concatenate([ctx, x], axis=1)
    w_in_b = w_in.astype(BF16)
    w_branch_b = w_branch.astype(BF16)
    w_out_b = w_out.astype(BF16)
    w_gu_b = ffn_w_gu.astype(BF16)
    w_down_b = ffn_w_down.astype(BF16)
    sgu_w_b = sgu_w.astype(BF16)
    row = lambda a: a.reshape(1, -1)

    for l in range(depth):
        last = l == depth - 1
        gates, q, k, v, p_rwkv, p_sgu = _inproj(xs, mods, w_in_b, l, n_ctx_tiles)
        o_na = _na_attention(q, k, v, _na_bias_table(na_rpb[l]), n_ctx)
        prm = dict(
            mu_prev=row(rwkv_mu_prev[l]), mu_next=row(rwkv_mu_next[l]),
            w0=rwkv_w0[l], w2=rwkv_w2[l].reshape(2 * DECAY_LORA, BRANCH_WIDTH).astype(BF16),
            a0=rwkv_a0[l], a2=rwkv_a2[l].reshape(2 * DECAY_LORA, BRANCH_WIDTH).astype(BF16),
            g2=rwkv_g2[l].astype(BF16), k_k=row(rwkv_k_k[l]), k_a=row(rwkv_k_a[l]), r_k=row(rwkv_r_k[l]))
        q1, y0, m, n0, g, bv = _rwkv_prep(p_rwkv, prm, n_ctx_tiles)
        y_f, y_b = _rwkv_chain(q1, y0, m, n0, n_ctx)
        b_full = jnp.repeat(jnp.transpose(sgu_b[l]), BRANCH_WIDTH // SGU_GROUPS, axis=1)
        vecs = dict(gn_g=row(rwkv_gn_g[l]), gn_b=row(rwkv_gn_b[l]), sgu_ln_g=row(sgu_ln_g[l]),
                    sgu_ln_b=row(sgu_ln_b[l]), ln1_g=row(ln1_g[l]), ln1_b=row(ln1_b[l]), ln2_g=row(ln2_g[l]),
                    ln2_b=row(ln2_b[l]))
        xs = _post(xs, mods, gates, o_na, y_f, y_b, g, bv, p_sgu, vecs, sgu_w_b, b_full, w_branch_b, w_out_b,
                   w_gu_b, w_down_b, l, n_ctx_tiles, alpha, n_ctx_tiles if last else 0)
    return xs
```

```python
import functools

import jax
import jax.numpy as jnp
import numpy as np
from jax import lax
from jax.experimental import pallas as pl
from jax.experimental.pallas import tpu as pltpu

F32 = jnp.float32
BF16 = jnp.bfloat16

D_MODEL = 1024
GRID_W = 64
NA_HEADS = 8
HEAD_DIM = 64
NA_WIN_H = 8
NA_WIN_W = 16
BRANCH_WIDTH = 512
DECAY_LORA = 64
GATE_LORA = 128
RWKV_IN = 3 * BRANCH_WIDTH + 4 * DECAY_LORA + GATE_LORA
RWKV_GN_EPS = 64e-5
SGU_CHUNK = 128
SGU_GROUPS = 8
D_FF = 2816
LN_EPS = 1e-5
MASK_VALUE = -1e30

LANES = 128
TOKEN_TILE = 256
RWKV_CHUNK = 64
N_PAIRS = NA_HEADS // 2
QUAD = 4 * HEAD_DIM
HIGHEST = lax.Precision.HIGHEST
VMEM_LIMIT = 56 * 1024 * 1024

_NT = (((1,), (1,)), ((), ()))


def _params(sem, vmem=VMEM_LIMIT):
    return pltpu.CompilerParams(dimension_semantics=sem, vmem_limit_bytes=vmem)


def _dot(a, b):
    return jnp.dot(a.astype(BF16), b.astype(BF16), preferred_element_type=F32)


def _layer_norm(x, g, b, eps):
    mu = jnp.mean(x, axis=-1, keepdims=True)
    xc = x - mu
    var = jnp.mean(xc * xc, axis=-1, keepdims=True)
    return xc * lax.rsqrt(var + eps) * g + b


def _sigmoid(x):
    return 0.5 * jnp.tanh(0.5 * x) + 0.5


def _head_sum(x):
    rows = x.shape[0]
    lo = lax.broadcasted_iota(jnp.int32, (rows, LANES), 1) < HEAD_DIM
    out = []
    for p in range(x.shape[1] // LANES):
        xp = x[:, p * LANES:(p + 1) * LANES]
        s_lo = jnp.sum(jnp.where(lo, xp, 0.0), axis=-1, keepdims=True)
        s_hi = jnp.sum(jnp.where(lo, 0.0, xp), axis=-1, keepdims=True)
        out.append(jnp.where(lo, s_lo, s_hi))
    return jnp.concatenate(out, axis=1)


def _block_diag(xb):
    rows = xb.shape[0]
    lane = lax.broadcasted_iota(jnp.int32, (rows, LANES), 1)
    lo_b = jnp.where(lane < HEAD_DIM, 1.0, 0.0).astype(BF16)
    hi_b = jnp.where(lane >= HEAD_DIM, 1.0, 0.0).astype(BF16)
    zero_b = jnp.zeros((rows, LANES), BF16)
    t0, t1 = xb[:, :LANES], xb[:, LANES:]
    return jnp.concatenate([
        jnp.concatenate([t0 * lo_b, zero_b], axis=1), jnp.concatenate([t0 * hi_b, zero_b], axis=1),
        jnp.concatenate([zero_b, t1 * lo_b], axis=1), jnp.concatenate([zero_b, t1 * hi_b], axis=1)], axis=0)


def _ada_kernel(c_ref, w_ref, b_ref, o_ref):
    c = c_ref[...]
    o_ref[...] = _dot(c * _sigmoid(c), w_ref[...]) + b_ref[...]


def _ada_all(cond, w_ada, b_ada):
    depth, d, n = w_ada.shape
    n_mod = n // d
    return pl.pallas_call(
        _ada_kernel,
        grid=(depth, n_mod),
        in_specs=[
            pl.BlockSpec((8, d), lambda l, i: (0, 0)),
            pl.BlockSpec((None, d, d), lambda l, i: (l, 0, i)),
            pl.BlockSpec((None, 1, d), lambda l, i: (l, 0, i)),
        ],
        out_specs=pl.BlockSpec((None, None, 8, d), lambda l, i: (l, i, 0, 0)),
        out_shape=jax.ShapeDtypeStruct((depth, n_mod, 8, d), F32),
        compiler_params=_params(("parallel", "parallel")),
        name="adaln",
    )(cond, w_ada, b_ada.reshape(depth, 1, n))


_IN_SPLITS = (0, 3 * D_MODEL, 3 * D_MODEL + 512, 3 * D_MODEL + 1024, 3 * D_MODEL + 1536,
              3 * D_MODEL + 1536 + RWKV_IN, 3 * D_MODEL + 1536 + RWKV_IN + 2 * BRANCH_WIDTH)
_IN_COL_CHUNK = 512


def _inproj_kernel(x_ref, mod_ref, w_ref, gates_ref, q_ref, k_ref, v_ref, pr_ref, ps_ref):
    h = (x_ref[...] * (1.0 + mod_ref[1:2, :]) + mod_ref[0:1, :]).astype(BF16)
    outs = (gates_ref, q_ref, k_ref, v_ref, pr_ref, ps_ref)
    for idx, o_ref in enumerate(outs):
        lo, hi = _IN_SPLITS[idx], _IN_SPLITS[idx + 1]
        scale = HEAD_DIM ** -0.5 if idx == 1 else None
        for c0 in range(lo, hi, _IN_COL_CHUNK):
            c1 = min(c0 + _IN_COL_CHUNK, hi)
            r = jnp.dot(h, w_ref[:, c0:c1], preferred_element_type=F32)
            if scale is not None:
                r = r * scale
            o_ref[:, c0 - lo:c1 - lo] = r.astype(o_ref.dtype)


def _mod_row(n_ctx_tiles, batch):
    return lambda b, j: jnp.where(j < n_ctx_tiles, batch, b)


def _inproj(x, mods, w_in, layer, n_ctx_tiles):
    batch, t, d = x.shape
    nt = t // TOKEN_TILE
    d_in = w_in.shape[-1]
    row = _mod_row(n_ctx_tiles, batch)
    widths = [_IN_SPLITS[i + 1] - _IN_SPLITS[i] for i in range(6)]
    dtypes = [BF16, BF16, BF16, BF16, F32, BF16]
    return pl.pallas_call(
        _inproj_kernel,
        grid=(batch, nt),
        in_specs=[
            pl.BlockSpec((None, TOKEN_TILE, d), lambda b, j: (b, j, 0)),
            pl.BlockSpec((None, None, 6, d), lambda b, j: (layer, row(b, j), 0, 0)),
            pl.BlockSpec((None, d, d_in), lambda b, j: (layer, 0, 0), pipeline_mode=pl.Buffered(1)),
        ],
        out_specs=[pl.BlockSpec((None, TOKEN_TILE, w), lambda b, j: (b, j, 0)) for w in widths],
        out_shape=[jax.ShapeDtypeStruct((batch, t, w), dt) for w, dt in zip(widths, dtypes)],
        compiler_params=_params(("parallel", "parallel")),
        name="inproj",
    )(x, mods, w_in)


def _na_bias_table(rpb):
    cols = np.arange(GRID_W)
    col0 = np.clip(cols - NA_WIN_W // 2, 0, GRID_W - NA_WIN_W)
    kc = np.arange(GRID_W)
    in_win = (kc[None, :] >= col0[:, None]) & (kc[None, :] < col0[:, None] + NA_WIN_W)
    dcol = kc[None, :] - cols[:, None] + NA_WIN_W - 1
    n_dcol = 2 * NA_WIN_W - 1
    onehot = ((dcol[None] == np.arange(n_dcol)[:, None, None]) & in_win[None]).astype(np.float32)
    toe = jnp.einsum("hrd,dck->hrck", rpb, jnp.asarray(onehot), precision=HIGHEST)
    toe = toe + jnp.asarray(np.where(in_win, 0.0, MASK_VALUE).astype(np.float32))
    tab = jnp.stack([toe[:, NA_WIN_H - 1 - e:2 * NA_WIN_H - 1 - e] for e in range(NA_WIN_H)])
    tab = jnp.transpose(tab, (0, 1, 3, 2, 4))
    return tab.reshape(NA_WIN_H, NA_HEADS, GRID_W, NA_WIN_H * GRID_W)


NA_ROWS_PER_STEP = 4


def _na_kernel(q_ref, k_ref, v_ref, *rest, n_ctx, rows):
    bias_refs, o_ref = rest[:NA_ROWS_PER_STEP], rest[NA_ROWS_PER_STEP]
    j = pl.program_id(1)
    ctx_steps = n_ctx // (GRID_W * NA_ROWS_PER_STEP)
    lo2 = lax.broadcasted_iota(jnp.int32, (2 * GRID_W, LANES), 1) < HEAD_DIM
    top = lax.broadcasted_iota(jnp.int32, (2 * GRID_W, LANES), 0) < GRID_W
    own_head = lo2 == top
    lo = lax.broadcasted_iota(jnp.int32, (GRID_W, LANES), 1) < HEAD_DIM
    n_win = NA_WIN_H * GRID_W
    pairs = [slice(p * LANES, (p + 1) * LANES) for p in range(N_PAIRS)]

    def attend(i, local):
        qs = slice(i * GRID_W, (i + 1) * GRID_W)
        if local:
            r = (j - ctx_steps) * NA_ROWS_PER_STEP + i
            row0 = jnp.clip(r - NA_WIN_H // 2, 0, rows - NA_WIN_H)
            start = pl.multiple_of(n_ctx + row0 * GRID_W, GRID_W)
        q2 = []
        for cs in pairs:
            qp = q_ref[qs, cs].astype(F32)
            q2.append(jnp.where(own_head, jnp.concatenate([qp, qp], axis=0), 0.0).astype(BF16))
        s_c = [lax.dot_general(q_, k_ref[0:n_ctx, cs], _NT, preferred_element_type=F32)
               for q_, cs in zip(q2, pairs)]
        mx = [jnp.max(s, axis=-1, keepdims=True) for s in s_c]
        if local:
            s_l = [lax.dot_general(q_, k_ref[pl.ds(start, n_win), cs], _NT, preferred_element_type=F32)
                   + bias_refs[i][p] for p, (q_, cs) in enumerate(zip(q2, pairs))]
            mx = [jnp.maximum(m, jnp.max(s, axis=-1, keepdims=True)) for m, s in zip(mx, s_l)]
            p_l = [jnp.exp(s - m) for s, m in zip(s_l, mx)]
        p_c = [jnp.exp(s - m) for s, m in zip(s_c, mx)]
        den = [jnp.sum(x, axis=-1, keepdims=True) for x in p_c]
        acc = [jnp.dot(x.astype(BF16), v_ref[0:n_ctx, cs], preferred_element_type=F32)
               for x, cs in zip(p_c, pairs)]
        if local:
            den = [d_ + jnp.sum(x, axis=-1, keepdims=True) for d_, x in zip(den, p_l)]
            acc = [a + jnp.dot(x.astype(BF16), v_ref[pl.ds(start, n_win), cs], preferred_element_type=F32)
                   for a, x, cs in zip(acc, p_l, pairs)]
        for a, d_, cs in zip(acc, den, pairs):
            o = a / d_
            o_ref[qs, cs] = jnp.where(lo, o[:GRID_W], o[GRID_W:])

    @pl.when(j < ctx_steps)
    def _():
        for i in range(NA_ROWS_PER_STEP):
            attend(i, False)

    @pl.when(j >= ctx_steps)
    def _():
        for i in range(NA_ROWS_PER_STEP):
            attend(i, True)


def _na_attention(q, k, v, bias, n_ctx):
    batch, t, w = q.shape
    rows = (t - n_ctx) // GRID_W
    tq = GRID_W * NA_ROWS_PER_STEP
    assert n_ctx % tq == 0 and rows % NA_ROWS_PER_STEP == 0
    ctx_steps = n_ctx // tq
    bias = bias.reshape(NA_WIN_H, N_PAIRS, 2 * GRID_W, NA_WIN_H * GRID_W)

    def bias_spec(i):
        def idx(b, j):
            r = jnp.maximum(j - ctx_steps, 0) * NA_ROWS_PER_STEP + i
            row0 = jnp.clip(r - NA_WIN_H // 2, 0, rows - NA_WIN_H)
            return (r - row0, 0, 0, 0)
        return pl.BlockSpec((None, N_PAIRS, 2 * GRID_W, NA_WIN_H * GRID_W), idx)

    return pl.pallas_call(
        functools.partial(_na_kernel, n_ctx=n_ctx, rows=rows),
        grid=(batch, t // tq),
        in_specs=[
            pl.BlockSpec((None, tq, w), lambda b, j: (b, j, 0)),
            pl.BlockSpec((None, t, w), lambda b, j: (b, 0, 0)),
            pl.BlockSpec((None, t, w), lambda b, j: (b, 0, 0)),
        ] + [bias_spec(i) for i in range(NA_ROWS_PER_STEP)],
        out_specs=pl.BlockSpec((None, tq, w), lambda b, j: (b, j, 0)),
        out_shape=jax.ShapeDtypeStruct((batch, t, w), F32),
        compiler_params=_params(("parallel", "arbitrary")),
        name="na_attention",
    )(q, k, v, *([bias] * NA_ROWS_PER_STEP))


def _gelu_tanh(x):
    return x * (0.5 * (1.0 + jnp.tanh(0.7978845608028654 * (x + 0.044715 * (x * x * x)))))


PREP_CHAINS_IN_FLIGHT = 16
_N_STAGE = 6


def _rwkv_prep_kernel(p_ref, prev_ref, next_ref, mup_ref, mun_ref, w0_ref, w2_ref, a0_ref, a2_ref, g2_ref,
                      kk_ref, ka_ref, rk_ref,
                      q1_ref, y0_ref, m_ref, n0_ref, g_ref, bv_ref, stage_ref, *, n_ctx_tiles, n_tiles):
    j = pl.program_id(1)
    tm = TOKEN_TILE
    w = BRANCH_WIDTH
    ch = RWKV_CHUNK

    p = p_ref[...]
    first = (j == 0) | (j == n_ctx_tiles)
    last = (j == n_ctx_tiles - 1) | (j == n_tiles - 1)
    prev_row = jnp.where(first, 0.0, prev_ref[7:8, :])
    next_row = jnp.where(last, 0.0, next_ref[0:1, :])
    row8 = lax.broadcasted_iota(jnp.int32, (8, 1), 0)
    down = pltpu.roll(p, 1, 0)
    up = pltpu.roll(p, tm - 1, 0)
    p_prev = jnp.concatenate([jnp.where(row8 == 0, prev_row, down[:8]), down[8:]], axis=0)
    p_next = jnp.concatenate([up[:tm - 8], jnp.where(row8 == 7, next_row, up[tm - 8:])], axis=0)
    ps = (1.0 - mup_ref[...] - mun_ref[...]) * p + mup_ref[...] * p_prev + mun_ref[...] * p_next

    r = ps[:, 0:w]
    k = ps[:, w:2 * w]
    v = ps[:, 2 * w:3 * w]
    wc = jnp.tanh(ps[:, 3 * w:3 * w + LANES])
    ac = ps[:, 3 * w + LANES:3 * w + 2 * LANES]
    gc = ps[:, 3 * w + 2 * LANES:3 * w + 3 * LANES]

    g_ref[...] = _dot(_sigmoid(gc), g2_ref[...])
    kk = k * kk_ref[...]
    kk = kk * lax.rsqrt(_head_sum(kk * kk) + 1e-12)

    lo_dir = lax.broadcasted_iota(jnp.int32, (tm, LANES), 1) < DECAY_LORA
    ti = lax.broadcasted_iota(jnp.int32, (tm, tm), 0)
    si = lax.broadcasted_iota(jnp.int32, (tm, tm), 1)
    same_chunk = (ti // ch) == (si // ch)
    ci = lax.broadcasted_iota(jnp.int32, (ch, QUAD), 0)
    cj = lax.broadcasted_iota(jnp.int32, (ch, QUAD), 1) % HEAD_DIM
    eye_w = jnp.where(ci == cj, 1.0, 0.0)
    bi =lax.broadcasted_iota(jnp.int32, (QUAD, QUAD), 0) // HEAD_DIM
    bj = lax.broadcasted_iota(jnp.int32, (QUAD, QUAD), 1) // HEAD_DIM
    maskb = jnp.where(bi == bj, 1.0, 0.0).astype(BF16)
    eye_b = eye_w.astype(BF16)

    kd_sum = jnp.zeros((tm, w), F32)
    for d in range(2):
        md = lo_dir if d == 0 else ~lo_dir
        z = w0_ref[d:d + 1, :] + _dot(jnp.where(md, wc, 0.0), w2_ref[...])
        lw = -float(np.exp(-0.5)) * _sigmoid(z)
        a = _sigmoid(a0_ref[d:d + 1, :] + _dot(jnp.where(md, ac, 0.0), a2_ref[...]))
        kd = k * (1.0 + (a - 1.0) * ka_ref[...])
        kd_sum = kd_sum + kd
        beta = kk * a
        before = (si <= ti) if d == 0 else (si >= ti)
        tri = jnp.where(same_chunk & before, 1.0, 0.0).astype(BF16)
        lw_hi = lw.astype(BF16)
        rem = lw - lw_hi.astype(F32)
        lw_mid = rem.astype(BF16)
        lw_lo = (rem - lw_mid.astype(F32)).astype(BF16)
        parts = jnp.dot(tri, jnp.concatenate([lw_hi, lw_mid, lw_lo], axis=1), preferred_element_type=F32)
        cum = (parts[:, 2 * w:] + parts[:, w:2 * w]) + parts[:, :w]
        last_step = ch - 1 if d == 0 else 0
        tot = jnp.concatenate(
            [jnp.broadcast_to(cum[c * ch + last_step:c * ch + last_step + 1], (ch, w)) for c in range(tm // ch)],
            axis=0)
        e_neg = jnp.exp(-cum)
        stage_ref[d, 0] = -kk * jnp.exp(cum - lw)
        stage_ref[d, 1] = r * jnp.exp(cum)
        stage_ref[d, 2] = beta * e_neg
        stage_ref[d, 3] = kd * e_neg
        stage_ref[d, 4] = v
        stage_ref[d, 5] = jnp.exp(tot)

    strict = ((cj < ci), (cj > ci))
    incl = ((cj <= ci), (cj >= ci))

    def bd(x):
        return _block_diag(x.astype(BF16))

    def wide_dot(lhs, rhs):
        return jnp.dot(lhs.astype(BF16), rhs, preferred_element_type=F32)

    def solve(chains):
        n_ch = len(chains)
        dirs = [d for d, _, _ in chains]

        def load(i):
            return [stage_ref[d, i, c * ch:(c + 1) * ch, qd * QUAD:(qd + 1) * QUAD] for d, c, qd in chains]

        ab, rb, bt, kt = load(0), load(1), load(2), load(3)
        lhs = [jnp.concatenate([a_.astype(BF16), r_.astype(BF16), eye_b], axis=0) for a_, r_ in zip(ab, rb)]
        g1 = [lax.dot_general(l_, bd(x), _NT, preferred_element_type=F32) for l_, x in zip(lhs, bt)]
        g2 = [lax.dot_general(l_, bd(x), _NT, preferred_element_type=F32) for l_, x in zip(lhs, kt)]
        diag_w = [jnp.where(eye_w > 0.0, stage_ref[d, 5, c * ch:c * ch + 1, qd * QUAD:(qd + 1) * QUAD], 0.0)
                  for d, c, qd in chains]
        scale = wide_dot(jnp.concatenate(diag_w, axis=0), maskb)
        scale = [scale[i * ch:(i + 1) * ch] for i in range(n_ch)]
        apow = [jnp.where(strict[d], g[:ch], 0.0) for d, g in zip(dirs, g1)]
        h_rb = [jnp.where(incl[d], g[ch:2 * ch], 0.0) for d, g in zip(dirs, g1)]
        j_bh = [s_ * g[2 * ch:] for s_, g in zip(scale, g1)]
        a_ak = [jnp.where(strict[d], g[:ch], 0.0) for d, g in zip(dirs, g2)]
        tail = [jnp.concatenate([jnp.where(incl[d], g[ch:2 * ch], 0.0), s_ * g[2 * ch:]], axis=0)
                for d, s_, g in zip(dirs, scale, g2)]
        for step in range(6):
            grow = step < 5
            stacked = [jnp.concatenate(([p_] if grow else []) + [h_, j_], axis=0)
                       for p_, h_, j_ in zip(apow, h_rb, j_bh)]
            out = [wide_dot(l_, bd(p_)) for l_, p_ in zip(stacked, apow)]
            off = ch if grow else 0
            if grow:
                apow = [o[:ch] for o in out]
            h_rb = [x + o[off:off + ch] for x, o in zip(h_rb, out)]
            j_bh = [x + o[off + ch:] for x, o in zip(j_bh, out)]
        hj = [jnp.concatenate([h_, j_], axis=0).astype(BF16) for h_, j_ in zip(h_rb, j_bh)]
        x12 = [jnp.dot(l_, jnp.concatenate([bd(a_), bd(k_)], axis=1), preferred_element_type=F32)
               for l_, a_, k_ in zip(hj, ab, a_ak)]
        vv = load(4)
        x3 = [wide_dot(x[:, QUAD:] + t_, bd(v_)) for x, t_, v_ in zip(x12, tail, vv)]
        for i, (d, c, qd) in enumerate(chains):
            rs = slice(c * ch, (c + 1) * ch)
            cs = slice(qd * QUAD, (qd + 1) * QUAD)
            q1_ref[d, rs, cs] = (rb[i] + x12[i][:ch, :QUAD]).astype(BF16)
            y0_ref[d, rs, cs] = x3[i][:ch]
            m_ref[d, c, :, cs] = x12[i][ch:, :QUAD] + diag_w[i]
            n0_ref[d, c, :, cs] = x3[i][ch:]

    all_chains = [(d, c, qd) for c in range(tm // ch) for d in range(2) for qd in range(w // QUAD)]
    for g0 in range(0, len(all_chains), PREP_CHAINS_IN_FLIGHT):
        solve(all_chains[g0:g0 + PREP_CHAINS_IN_FLIGHT])

    bv_ref[...] = _head_sum(r * kd_sum * rk_ref[...]) * v


def _rwkv_prep(p_rwkv, prm, n_ctx_tiles):
    batch, t, wp = p_rwkv.shape
    nt = t // TOKEN_TILE
    cpt = TOKEN_TILE // RWKV_CHUNK
    nc = t // RWKV_CHUNK
    halo = TOKEN_TILE // 8
    w = BRANCH_WIDTH
    c2 = lambda b, j: (0, 0)
    vec = lambda n: pl.BlockSpec((1, n), c2)
    tok = pl.BlockSpec((None, TOKEN_TILE, w), lambda b, j: (b, j, 0))
    dtok = pl.BlockSpec((None, 2, TOKEN_TILE, w), lambda b, j: (b, 0, j, 0))
    mat = pl.BlockSpec((None, 2, cpt, HEAD_DIM, w), lambda b, j: (b, 0, j, 0, 0))
    return pl.pallas_call(
        functools.partial(_rwkv_prep_kernel, n_ctx_tiles=n_ctx_tiles, n_tiles=nt),
        grid=(batch, nt),
        in_specs=[
            pl.BlockSpec((None, TOKEN_TILE, wp), lambda b, j: (b, j, 0)),
            pl.BlockSpec((None, 8, wp), lambda b, j: (b, jnp.maximum(j * halo - 1, 0), 0)),
            pl.BlockSpec((None, 8, wp), lambda b, j: (b, jnp.minimum((j + 1) * halo, t // 8 - 1), 0)),
            vec(wp), vec(wp),
            pl.BlockSpec((2, w), c2), pl.BlockSpec((2 * DECAY_LORA, w), c2),
            pl.BlockSpec((2, w), c2), pl.BlockSpec((2 * DECAY_LORA, w), c2),
            pl.BlockSpec((GATE_LORA, w), c2),
            vec(w), vec(w), vec(w),
        ],
        out_specs=[dtok, dtok, mat, mat, tok, tok],
        out_shape=[
            jax.ShapeDtypeStruct((batch, 2, t, w), BF16),
            jax.ShapeDtypeStruct((batch, 2, t, w), F32),
            jax.ShapeDtypeStruct((batch, 2, nc, HEAD_DIM, w), F32),
            jax.ShapeDtypeStruct((batch, 2, nc, HEAD_DIM, w), F32),
            jax.ShapeDtypeStruct((batch, t, w), F32),
            jax.ShapeDtypeStruct((batch, t, w), F32),
        ],
        scratch_shapes=[pltpu.VMEM((2, _N_STAGE, TOKEN_TILE, w), F32)],
        compiler_params=_params(("parallel", "parallel")),
        name="rwkv_prep",
    )(p_rwkv, p_rwkv, p_rwkv, prm["mu_prev"], prm["mu_next"], prm["w0"], prm["w2"], prm["a0"], prm["a2"],
      prm["g2"], prm["k_k"], prm["k_a"], prm["r_k"])


def _rwkv_chain_kernel(q1f_ref, mf_ref, n0f_ref, q1b_ref, mb_ref, n0b_ref, yf_ref, yb_ref, s_ref):
    @pl.when(pl.program_id(0) == 0)
    def _():
        s_ref[...] = jnp.zeros_like(s_ref)

    batch = q1f_ref.shape[0]
    ch = RWKV_CHUNK
    dirs = ((q1f_ref, mf_ref, n0f_ref, yf_ref), (q1b_ref, mb_ref, n0b_ref, yb_ref))

    def split(x):
        hi = x.astype(BF16)
        return hi, (x - hi.astype(F32)).astype(BF16)

    chains = [(d, b, slice(qd * QUAD, (qd + 1) * QUAD))
              for d in range(2) for b in range(batch) for qd in range(BRANCH_WIDTH // QUAD)]
    outs = []
    for d, b, cs in chains:
        q1_ref, m_ref, _, _ = dirs[d]
        s_hi, s_lo = split(s_ref[d, b, :, cs])
        m_hi, m_lo = split(m_ref[b, :, cs])
        lhs = jnp.concatenate([q1_ref[b, :, cs], m_hi, m_lo], axis=0)
        o = jnp.dot(lhs, _block_diag(s_hi), preferred_element_type=F32)
        o_lo = jnp.dot(m_hi, _block_diag(s_lo), preferred_element_type=F32)
        outs.append((o[:ch], o[ch:2 * ch] + o[2 * ch:] + o_lo))
    for (d, b, cs), (y, s_new) in zip(chains, outs):
        _, _, n0_ref, y_ref = dirs[d]
        y_ref[b, :, cs] = y
        s_ref[d, b, :, cs] = s_new + n0_ref[b, :, cs]


def _rwkv_chain(q1, m, n0, n_ctx):
    batch, _, t, w = q1.shape
    nc = t // RWKV_CHUNK
    ncc = n_ctx // RWKV_CHUNK

    def chunk(d, s):
        return s if d == 0 else jnp.where(s < ncc, ncc - 1 - s, nc - 1 - (s - ncc))

    in_specs = []
    for d in range(2):
        tok = pl.BlockSpec((batch, None, RWKV_CHUNK, w), lambda s, d=d: (0, d, chunk(d, s), 0))
        mat = pl.BlockSpec((batch, None, None, HEAD_DIM, w), lambda s, d=d: (0, d, chunk(d, s), 0, 0))
        in_specs += [tok, mat, mat]
    out_specs = [pl.BlockSpec((batch, RWKV_CHUNK, w), lambda s, d=d: (0, chunk(d, s), 0)) for d in range(2)]
    return pl.pallas_call(
        _rwkv_chain_kernel,
        grid=(nc,),
        in_specs=in_specs,
        out_specs=out_specs,
        out_shape=[jax.ShapeDtypeStruct((batch, t, w), F32)] * 2,
        scratch_shapes=[pltpu.VMEM((2, batch, HEAD_DIM, w), F32)],
        compiler_params=_params(("arbitrary",)),
        name="rwkv_chain",
    )(q1, m, n0, q1, m, n0)


_FF_CHUNK = 256


def _post_kernel(x_ref, mod_ref, gates_ref, ona_ref, yf_ref, yb_ref, y0_ref, g_ref, bv_ref, psgu_ref,
                 gng_ref, gnb_ref, slng_ref, slnb_ref, sw_ref, sb_ref,
                 wb_ref, wo_ref, ln1g_ref, ln1b_ref, wgu_ref, wd_ref, ln2g_ref, ln2b_ref,
                 o_ref, osgu_ref, act_ref, *, alpha):
    w = BRANCH_WIDTH
    d = D_MODEL

    gl = _gelu_tanh(psgu_ref[...].astype(F32))
    u = gl[:, :w]
    v = _layer_norm(gl[:, w:], slng_ref[...], slnb_ref[...], LN_EPS).astype(BF16)
    lo = lax.broadcasted_iota(jnp.int32, (SGU_CHUNK, LANES), 1) < HEAD_DIM
    for ck in range(TOKEN_TILE // SGU_CHUNK):
        rs = slice(ck * SGU_CHUNK, (ck + 1) * SGU_CHUNK)
        for p in range(N_PAIRS):
            cs = slice(p * LANES, (p + 1) * LANES)
            vp = v[rs, cs]
            r0 = jnp.dot(sw_ref[2 * p], vp, preferred_element_type=F32)
            r1 = jnp.dot(sw_ref[2 * p + 1], vp, preferred_element_type=F32)
            osgu_ref[rs, cs] = (u[rs, cs] * (jnp.where(lo, r0, r1) + sb_ref[:, cs])).astype(BF16)

    y = (yf_ref[...] + y0_ref[0]) + (yb_ref[...] + y0_ref[1])
    mu = _head_sum(y) * (1.0 / HEAD_DIM)
    yc = y - mu
    var = _head_sum(yc * yc) * (1.0 / HEAD_DIM)
    y = yc * lax.rsqrt(var + RWKV_GN_EPS) * gng_ref[...] + gnb_ref[...]
    o_rwkv = (y + bv_ref[...]) * g_ref[...]

    acc = _sigmoid(gates_ref[:, 0:d].astype(F32)) * _dot(ona_ref[...], wb_ref[0])
    acc = acc + _sigmoid(gates_ref[:, d:2 * d].astype(F32)) * _dot(o_rwkv, wb_ref[1])
    acc = acc + _sigmoid(gates_ref[:, 2 * d:3 * d].astype(F32)) * jnp.dot(
        osgu_ref[...], wb_ref[2], preferred_element_type=F32)
    z = _dot(acc, wo_ref[...])
    x1 = _layer_norm(alpha * x_ref[...] + mod_ref[2:3, :] * z, ln1g_ref[...], ln1b_ref[...], LN_EPS)

    h = (x1 * (1.0 + mod_ref[4:5, :]) + mod_ref[3:4, :]).astype(BF16)
    for c0 in range(0, D_FF, _FF_CHUNK):
        gate = jnp.dot(h, wgu_ref[:, c0:c0 + _FF_CHUNK], preferred_element_type=F32)
        up = jnp.dot(h, wgu_ref[:, D_FF + c0:D_FF + c0 + _FF_CHUNK], preferred_element_type=F32)
        act_ref[:, c0:c0 + _FF_CHUNK] = (gate * _sigmoid(gate) * up).astype(BF16)
    z = jnp.dot(act_ref[...], wd_ref[...], preferred_element_type=F32)
    o_ref[...] = _layer_norm(alpha * x1 + mod_ref[5:6, :] * z, ln2g_ref[...], ln2b_ref[...], LN_EPS)


def _post(x, mods, gates, o_na, y_f, y_b, y0, g, bv, p_sgu, vecs, sgu_w, sgu_b_full, w_branch, w_out, w_gu, w_down,
          layer, n_ctx_tiles, alpha, skip_tiles):
    batch, t, d = x.shape
    nt = t // TOKEN_TILE - skip_tiles
    w = BRANCH_WIDTH
    row = _mod_row(n_ctx_tiles, batch)
    c2 = lambda b, j: (0, 0)
    tok = lambda n: pl.BlockSpec((None, TOKEN_TILE, n), lambda b, j: (b, j + skip_tiles, 0))
    vw = pl.BlockSpec((1, w), c2)
    vd = pl.BlockSpec((1, d), c2)
    once = dict(pipeline_mode=pl.Buffered(1))
    return pl.pallas_call(
        functools.partial(_post_kernel, alpha=alpha),
        grid=(batch, nt),
        in_specs=[
            tok(d),
            pl.BlockSpec((None, None, 6, d), lambda b, j: (layer, row(b, j + skip_tiles), 0, 0)),
            tok(3 * d), tok(w), tok(w), tok(w),
            pl.BlockSpec((None, 2, TOKEN_TILE, w), lambda b, j: (b, 0, j + skip_tiles, 0)),
            tok(w), tok(w), tok(2 * w),
            vw, vw, vw, vw,
            pl.BlockSpec((None, SGU_GROUPS, SGU_CHUNK, SGU_CHUNK), lambda b, j: (layer, 0, 0, 0)),
            pl.BlockSpec((SGU_CHUNK, w), c2),
            pl.BlockSpec((None, 3, w, d), lambda b, j: (layer, 0, 0, 0), **once),
            pl.BlockSpec((None, d, d), lambda b, j: (layer, 0, 0), **once),
            vd, vd,
            pl.BlockSpec((None, d, 2 * D_FF), lambda b, j: (layer, 0, 0), **once),
            pl.BlockSpec((None, D_FF, d), lambda b, j: (layer, 0, 0), **once),
            vd, vd,
        ],
        out_specs=pl.BlockSpec((None, TOKEN_TILE, d), lambda b, j: (b, j, 0)),
        out_shape=jax.ShapeDtypeStruct((batch, nt * TOKEN_TILE, d), F32),
        scratch_shapes=[pltpu.VMEM((TOKEN_TILE, w), BF16), pltpu.VMEM((TOKEN_TILE, D_FF), BF16)],
        compiler_params=_params(("parallel", "parallel")),
        name="post",
    )(x, mods, gates, o_na, y_f, y_b, y0, g, bv, p_sgu, vecs["gn_g"], vecs["gn_b"], vecs["sgu_ln_g"],
      vecs["sgu_ln_b"], sgu_w, sgu_b_full, w_branch, w_out, vecs["ln1_g"], vecs["ln1_b"], w_gu, w_down,
      vecs["ln2_g"], vecs["ln2_b"])


def kernel(x, c, ctx, c_ctx, w_ada, b_ada, w_in, na_rpb, rwkv_mu_prev, rwkv_mu_next, rwkv_w0, rwkv_w2, rwkv_a0,
           rwkv_a2, rwkv_g2, rwkv_k_k, rwkv_k_a, rwkv_r_k, rwkv_gn_g, rwkv_gn_b, sgu_ln_g, sgu_ln_b, sgu_w, sgu_b,
           w_branch, w_out, ln1_g, ln1_b, ln2_g, ln2_b, ffn_w_gu, ffn_w_down):
    batch, seq, d = x.shape
    n_ctx = ctx.shape[1]
    depth = w_ada.shape[0]
    assert d == D_MODEL and batch <= 7
    assert n_ctx % TOKEN_TILE == 0 and seq % TOKEN_TILE == 0
    assert seq // GRID_W >= NA_WIN_H
    n_ctx_tiles = n_ctx // TOKEN_TILE
    alpha = float((2 * depth) ** 0.25)

    cond = jnp.zeros((8, d), F32).at[:batch].set(c).at[batch].set(c_ctx)
    mods = jnp.transpose(_ada_all(cond, w_ada, b_ada), (0, 2, 1, 3))

    xs = jnp.concatenate([ctx, x], axis=1)
    w_in_b = w_in.astype(BF16)
    w_branch_b = w_branch.astype(BF16)
    w_out_b = w_out.astype(BF16)
    w_gu_b = ffn_w_gu.astype(BF16)
    w_down_b = ffn_w_down.astype(BF16)
    sgu_w_b = sgu_w.astype(BF16)
    row = lambda a: a.reshape(1, -1)

    for l in range(depth):
        last = l == depth - 1
        gates, q, k, v, p_rwkv, p_sgu = _inproj(xs, mods, w_in_b, l, n_ctx_tiles)
        o_na = _na_attention(q, k, v, _na_bias_table(na_rpb[l]), n_ctx)
        prm = dict(
            mu_prev=row(rwkv_mu_prev[l]), mu_next=row(rwkv_mu_next[l]),
            w0=rwkv_w0[l], w2=rwkv_w2[l].reshape(2 * DECAY_LORA, BRANCH_WIDTH).astype(BF16),
            a0=rwkv_a0[l], a2=rwkv_a2[l].reshape(2 * DECAY_LORA, BRANCH_WIDTH).astype(BF16),
            g2=rwkv_g2[l].astype(BF16), k_k=row(rwkv_k_k[l]), k_a=row(rwkv_k_a[l]), r_k=row(rwkv_r_k[l]))
        q1, y0, m, n0, g, bv = _rwkv_prep(p_rwkv, prm, n_ctx_tiles)
        y_f, y_b = _rwkv_chain(q1, m, n0, n_ctx)
        b_full = jnp.repeat(jnp.transpose(sgu_b[l]), BRANCH_WIDTH // SGU_GROUPS, axis=1)
        vecs = dict(gn_g=row(rwkv_gn_g[l]), gn_b=row(rwkv_gn_b[l]), sgu_ln_g=row(sgu_ln_g[l]),
                    sgu_ln_b=row(sgu_ln_b[l]), ln1_g=row(ln1_g[l]), ln1_b=row(ln1_b[l]), ln2_g=row(ln2_g[l]),
                    ln2_b=row(ln2_b[l]))
        xs = _post(xs, mods, gates, o_na, y_f, y_b, y0, g, bv, p_sgu, vecs, sgu_w_b, b_full, w_branch_b, w_out_b,
                   w_gu_b, w_down_b, l, n_ctx_tiles, alpha, n_ctx_tiles if last else 0)
    return xs
```

```python
import functools

import jax
import jax.numpy as jnp
import numpy as np
from jax import lax
from jax.experimental import pallas as pl
from jax.experimental.pallas import tpu as pltpu

F32 = jnp.float32
BF16 = jnp.bfloat16

D_MODEL = 1024
GRID_W = 64
NA_HEADS = 8
HEAD_DIM = 64
NA_WIN_H = 8
NA_WIN_W = 16
BRANCH_WIDTH = 512
DECAY_LORA = 64
GATE_LORA = 128
RWKV_IN = 3 * BRANCH_WIDTH + 4 * DECAY_LORA + GATE_LORA
RWKV_GN_EPS = 64e-5
SGU_CHUNK = 128
SGU_GROUPS = 8
D_FF = 2816
LN_EPS = 1e-5
MASK_VALUE = -1e30

LANES = 128
TOKEN_TILE = 256
RWKV_CHUNK = 64
N_PAIRS = NA_HEADS // 2
QUAD = 4 * HEAD_DIM
HIGHEST = lax.Precision.HIGHEST
VMEM_LIMIT = 56 * 1024 * 1024

_NT = (((1,), (1,)), ((), ()))


def _params(sem, vmem=VMEM_LIMIT):
    return pltpu.CompilerParams(dimension_semantics=sem, vmem_limit_bytes=vmem)


def _dot(a, b):
    return jnp.dot(a.astype(BF16), b.astype(BF16), preferred_element_type=F32)


def _layer_norm(x, g, b, eps):
    mu = jnp.mean(x, axis=-1, keepdims=True)
    xc = x - mu
    var = jnp.mean(xc * xc, axis=-1, keepdims=True)
    return xc * lax.rsqrt(var + eps) * g + b


def _sigmoid(x):
    return 0.5 * jnp.tanh(0.5 * x) + 0.5


def _head_sum(x):
    rows = x.shape[0]
    lo = lax.broadcasted_iota(jnp.int32, (rows, LANES), 1) < HEAD_DIM
    out = []
    for p in range(x.shape[1] // LANES):
        xp = x[:, p * LANES:(p + 1) * LANES]
        s_lo = jnp.sum(jnp.where(lo, xp, 0.0), axis=-1, keepdims=True)
        s_hi = jnp.sum(jnp.where(lo, 0.0, xp), axis=-1, keepdims=True)
        out.append(jnp.where(lo, s_lo, s_hi))
    return jnp.concatenate(out, axis=1)


def _block_diag(xb):
    rows = xb.shape[0]
    lane = lax.broadcasted_iota(jnp.int32, (rows, LANES), 1)
    lo_b = jnp.where(lane < HEAD_DIM, 1.0, 0.0).astype(BF16)
    hi_b = jnp.where(lane >= HEAD_DIM, 1.0, 0.0).astype(BF16)
    zero_b = jnp.zeros((rows, LANES), BF16)
    t0, t1 = xb[:, :LANES], xb[:, LANES:]
    return jnp.concatenate([
        jnp.concatenate([t0 * lo_b, zero_b], axis=1), jnp.concatenate([t0 * hi_b, zero_b], axis=1),
        jnp.concatenate([zero_b, t1 * lo_b], axis=1), jnp.concatenate([zero_b, t1 * hi_b], axis=1)], axis=0)


def _ada_kernel(c_ref, w_ref, b_ref, o_ref):
    c = c_ref[...]
    o_ref[...] = _dot(c * _sigmoid(c), w_ref[...]) + b_ref[...]


def _ada_all(cond, w_ada, b_ada):
    depth, d, n = w_ada.shape
    n_mod = n // d
    return pl.pallas_call(
        _ada_kernel,
        grid=(depth, n_mod),
        in_specs=[
            pl.BlockSpec((8, d), lambda l, i: (0, 0)),
            pl.BlockSpec((None, d, d), lambda l, i: (l, 0, i)),
            pl.BlockSpec((None, 1, d), lambda l, i: (l, 0, i)),
        ],
        out_specs=pl.BlockSpec((None, None, 8, d), lambda l, i: (l, i, 0, 0)),
        out_shape=jax.ShapeDtypeStruct((depth, n_mod, 8, d), F32),
        compiler_params=_params(("parallel", "parallel")),
        name="adaln",
    )(cond, w_ada, b_ada.reshape(depth, 1, n))


_IN_SPLITS = (0, 3 * D_MODEL, 3 * D_MODEL + 512, 3 * D_MODEL + 1024, 3 * D_MODEL + 1536,
              3 * D_MODEL + 1536 + RWKV_IN, 3 * D_MODEL + 1536 + RWKV_IN + 2 * BRANCH_WIDTH)
_IN_COL_CHUNK = 512
_HALO = 8
_N_STAGE = 4


def _inproj_kernel(x_ref, xp_ref, xn_ref, mod_ref, w_ref, mup_ref, mun_ref, w0_ref, w2_ref, a0_ref, a2_ref,
                   g2_ref, kk_ref, ka_ref, rk_ref,
                   gates_ref, q_ref, k_ref, v_ref, ps_ref, stage_ref, rv_ref, dg_ref, g_ref, bv_ref, p_scr,
                   *, n_ctx_tiles, n_tiles):
    j = pl.program_id(1)
    tm = TOKEN_TILE
    w = BRANCH_WIDTH
    ch = RWKV_CHUNK
    shift = mod_ref[0:1, :]
    scale1 = 1.0 + mod_ref[1:2, :]
    h = (x_ref[...] * scale1 + shift).astype(BF16)

    h_ext = jnp.concatenate([(xp_ref[...] * scale1 + shift).astype(BF16), h,
                             (xn_ref[...] * scale1 + shift).astype(BF16)], axis=0)
    lo_r, hi_r = _IN_SPLITS[4], _IN_SPLITS[5]
    for c0 in range(lo_r, hi_r, _IN_COL_CHUNK):
        c1 = min(c0 + _IN_COL_CHUNK, hi_r)
        p_scr[:, c0 - lo_r:c1 - lo_r] = jnp.dot(h_ext, w_ref[:, c0:c1], preferred_element_type=F32)

    def project():
        outs = ((0, gates_ref), (1, q_ref), (2, k_ref), (3, v_ref), (5, ps_ref))
        for idx, o_ref in outs:
            lo, hi = _IN_SPLITS[idx], _IN_SPLITS[idx + 1]
            scale = HEAD_DIM ** -0.5 if idx == 1 else None
            for c0 in range(lo, hi, _IN_COL_CHUNK):
                c1 = min(c0 + _IN_COL_CHUNK, hi)
                r = jnp.dot(h, w_ref[:, c0:c1], preferred_element_type=F32)
                if scale is not None:
                    r = r * scale
                o_ref[:, c0 - lo:c1 - lo] = r.astype(o_ref.dtype)
                yield

    lo_dir = lax.broadcasted_iota(jnp.int32, (tm, LANES), 1) < DECAY_LORA
    ti = lax.broadcasted_iota(jnp.int32, (tm, tm), 0)
    si = lax.broadcasted_iota(jnp.int32, (tm, tm), 1)
    same_chunk = (ti // ch) == (si // ch)

    def stage_tile():
        p = p_scr[_HALO:_HALO + tm, :]
        first = (j == 0) | (j == n_ctx_tiles)
        last = (j == n_ctx_tiles - 1) | (j == n_tiles - 1)
        prev_row = jnp.where(first, 0.0, p_scr[_HALO - 1:_HALO, :])
        next_row = jnp.where(last, 0.0, p_scr[_HALO + tm:_HALO + tm + 1, :])
        row8 = lax.broadcasted_iota(jnp.int32, (8, 1), 0)
        down = pltpu.roll(p, 1, 0)
        up = pltpu.roll(p, tm - 1, 0)
        p_prev = jnp.concatenate([jnp.where(row8 == 0, prev_row, down[:8]), down[8:]], axis=0)
        p_next = jnp.concatenate([up[:tm - 8], jnp.where(row8 == 7, next_row, up[tm - 8:])], axis=0)
        ps = (1.0 - mup_ref[...] - mun_ref[...]) * p + mup_ref[...] * p_prev + mun_ref[...] * p_next
        yield
        r = ps[:, 0:w]
        k = ps[:, w:2 * w]
        v = ps[:, 2 * w:3 * w]
        wc = jnp.tanh(ps[:, 3 * w:3 * w + LANES])
        ac = ps[:, 3 * w + LANES:3 * w + 2 * LANES]
        gc = ps[:, 3 * w + 2 * LANES:3 * w + 3 * LANES]
        g_ref[...] = _dot(_sigmoid(gc), g2_ref[...])
        rv_ref[...] = v.astype(BF16)
        yield
        kk = k * kk_ref[...]
        kk = kk * lax.rsqrt(_head_sum(kk * kk) + 1e-12)
        yield
        kd_sum = jnp.zeros((tm, w), F32)
        for d in range(2):
            md = lo_dir if d == 0 else ~lo_dir
            z = w0_ref[d:d + 1, :] + _dot(jnp.where(md, wc, 0.0), w2_ref[...])
            lw = -float(np.exp(-0.5)) * _sigmoid(z)
            a = _sigmoid(a0_ref[d:d + 1, :] + _dot(jnp.where(md, ac, 0.0), a2_ref[...]))
            yield
            kd = k * (1.0 + (a - 1.0) * ka_ref[...])
            kd_sum = kd_sum + kd
            beta = kk * a
            before = (si <= ti) if d == 0 else (si >= ti)
            tri = jnp.where(same_chunk & before, 1.0, 0.0).astype(BF16)
            lw_hi = lw.astype(BF16)
            rem = lw - lw_hi.astype(F32)
            lw_mid = rem.astype(BF16)
            lw_lo = (rem - lw_mid.astype(F32)).astype(BF16)
            parts = jnp.dot(tri, jnp.concatenate([lw_hi, lw_mid, lw_lo], axis=1), preferred_element_type=F32)
            cum = (parts[:, 2 * w:] + parts[:, w:2 * w]) + parts[:, :w]
            yield
            e_neg = jnp.exp(-cum)
            stage_ref[d, 0] = (-kk * jnp.exp(cum - lw)).astype(BF16)
            stage_ref[d, 1] = (r * jnp.exp(cum)).astype(BF16)
            yield
            stage_ref[d, 2] = (beta * e_neg).astype(BF16)
            stage_ref[d, 3] = (kd * e_neg).astype(BF16)
            last_step = ch - 1 if d == 0 else 0
            for c in range(tm // ch):
                tot = cum[c * ch + last_step:c * ch + last_step + 1]
                dg_ref[d, c] = jnp.broadcast_to(jnp.exp(tot), (8, w))
            yield
        bv_ref[...] = _head_sum(r * kd_sum * rk_ref[...]) * v
        yield

    gens = [project(), stage_tile()]
    while gens:
        for gen in list(gens):
            if next(gen, StopIteration) is StopIteration:
                gens.remove(gen)


def _mod_row(n_ctx_tiles, batch):
    return lambda b, j: jnp.where(j < n_ctx_tiles, batch, b)


def _inproj(x, mods, w_in, prm, layer, n_ctx_tiles):
    batch, t, d = x.shape
    nt = t // TOKEN_TILE
    cpt = TOKEN_TILE // RWKV_CHUNK
    nc = t // RWKV_CHUNK
    d_in = w_in.shape[-1]
    w = BRANCH_WIDTH
    halo = TOKEN_TILE // _HALO
    row = _mod_row(n_ctx_tiles, batch)
    c2 = lambda b, j: (0, 0)
    vec = lambda n: pl.BlockSpec((1, n), c2)
    tok = lambda n: pl.BlockSpec((None, TOKEN_TILE, n), lambda b, j: (b, j, 0))
    widths = [_IN_SPLITS[i + 1] - _IN_SPLITS[i] for i in (0, 1, 2, 3, 5)]
    return pl.pallas_call(
        functools.partial(_inproj_kernel, n_ctx_tiles=n_ctx_tiles, n_tiles=nt),
        grid=(batch, nt),
        in_specs=[
            tok(d),
            pl.BlockSpec((None, _HALO, d), lambda b, j: (b, jnp.maximum(j * halo - 1, 0), 0)),
            pl.BlockSpec((None, _HALO, d), lambda b, j: (b, jnp.minimum((j + 1) * halo, t // _HALO - 1), 0)),
            pl.BlockSpec((None, None, 6, d), lambda b, j: (layer, row(b, j), 0, 0)),
            pl.BlockSpec((None, d, d_in), lambda b, j: (layer, 0, 0), pipeline_mode=pl.Buffered(1)),
            vec(RWKV_IN), vec(RWKV_IN),
            pl.BlockSpec((2, w), c2), pl.BlockSpec((2 * DECAY_LORA, w), c2),
            pl.BlockSpec((2, w), c2), pl.BlockSpec((2 * DECAY_LORA, w), c2),
            pl.BlockSpec((GATE_LORA, w), c2),
            vec(w), vec(w), vec(w),
        ],
        out_specs=[tok(n) for n in widths] + [
            pl.BlockSpec((None, 2, _N_STAGE, TOKEN_TILE, w), lambda b, j: (b, 0, 0, j, 0)),
            tok(w),
            pl.BlockSpec((None, 2, cpt, 8, w), lambda b, j: (b, 0, j, 0, 0)),
            tok(w), tok(w)],
        out_shape=[jax.ShapeDtypeStruct((batch, t, n), BF16) for n in widths] + [
            jax.ShapeDtypeStruct((batch, 2, _N_STAGE, t, w), BF16),
            jax.ShapeDtypeStruct((batch, t, w), BF16),
            jax.ShapeDtypeStruct((batch, 2, nc, 8, w), F32),
            jax.ShapeDtypeStruct((batch, t, w), F32),
            jax.ShapeDtypeStruct((batch, t, w), F32)],
        scratch_shapes=[pltpu.VMEM((TOKEN_TILE + 2 * _HALO, RWKV_IN), F32)],
        compiler_params=_params(("parallel", "parallel")),
        name="inproj",
    )(x, x, x, mods, w_in, prm["mu_prev"], prm["mu_next"], prm["w0"], prm["w2"], prm["a0"], prm["a2"],
      prm["g2"], prm["k_k"], prm["k_a"], prm["r_k"])


def _na_bias_table(rpb):
    cols = np.arange(GRID_W)
    col0 = np.clip(cols - NA_WIN_W // 2, 0, GRID_W - NA_WIN_W)
    kc = np.arange(GRID_W)
    in_win = (kc[None, :] >= col0[:, None]) & (kc[None, :] < col0[:, None] + NA_WIN_W)
    dcol = kc[None, :] - cols[:, None] + NA_WIN_W - 1
    n_dcol = 2 * NA_WIN_W - 1
    onehot = ((dcol[None] == np.arange(n_dcol)[:, None, None]) & in_win[None]).astype(np.float32)
    toe = jnp.einsum("hrd,dck->hrck", rpb, jnp.asarray(onehot), precision=HIGHEST)
    toe = toe + jnp.asarray(np.where(in_win, 0.0, MASK_VALUE).astype(np.float32))
    tab = jnp.stack([toe[:, NA_WIN_H - 1 - e:2 * NA_WIN_H - 1 - e] for e in range(NA_WIN_H)])
    tab = jnp.transpose(tab, (0, 1, 3, 2, 4))
    return tab.reshape(NA_WIN_H, NA_HEADS, GRID_W, NA_WIN_H * GRID_W)


NA_ROWS_PER_STEP = 4


def _na_kernel(q_ref, k_ref, v_ref, *rest, n_ctx, rows):
    bias_refs, o_ref = rest[:NA_ROWS_PER_STEP], rest[NA_ROWS_PER_STEP]
    j = pl.program_id(1)
    ctx_steps = n_ctx // (GRID_W * NA_ROWS_PER_STEP)
    lo2 = lax.broadcasted_iota(jnp.int32, (2 * GRID_W, LANES), 1) < HEAD_DIM
    top = lax.broadcasted_iota(jnp.int32, (2 * GRID_W, LANES), 0) < GRID_W
    own_head = lo2 == top
    lo = lax.broadcasted_iota(jnp.int32, (GRID_W, LANES), 1) < HEAD_DIM
    n_win = NA_WIN_H * GRID_W
    pairs = [slice(p * LANES, (p + 1) * LANES) for p in range(N_PAIRS)]

    def attend(i, local):
        qs = slice(i * GRID_W, (i + 1) * GRID_W)
        if local:
            r = (j - ctx_steps) * NA_ROWS_PER_STEP + i
            row0 = jnp.clip(r - NA_WIN_H // 2, 0, rows - NA_WIN_H)
            start = pl.multiple_of(n_ctx + row0 * GRID_W, GRID_W)
        q2 = []
        for cs in pairs:
            qp = q_ref[qs, cs].astype(F32)
            q2.append(jnp.where(own_head, jnp.concatenate([qp, qp], axis=0), 0.0).astype(BF16))
        s_c = [lax.dot_general(q_, k_ref[0:n_ctx, cs], _NT, preferred_element_type=F32)
               for q_, cs in zip(q2, pairs)]
        mx = [jnp.max(s, axis=-1, keepdims=True) for s in s_c]
        if local:
            s_l = [lax.dot_general(q_, k_ref[pl.ds(start, n_win), cs], _NT, preferred_element_type=F32)
                   + bias_refs[i][p] for p, (q_, cs) in enumerate(zip(q2, pairs))]
            mx = [jnp.maximum(m, jnp.max(s, axis=-1, keepdims=True)) for m, s in zip(mx, s_l)]
            p_l = [jnp.exp(s - m) for s, m in zip(s_l, mx)]
        p_c = [jnp.exp(s - m) for s, m in zip(s_c, mx)]
        den = [jnp.sum(x, axis=-1, keepdims=True) for x in p_c]
        acc = [jnp.dot(x.astype(BF16), v_ref[0:n_ctx, cs], preferred_element_type=F32)
               for x, cs in zip(p_c, pairs)]
        if local:
            den = [d_ + jnp.sum(x, axis=-1, keepdims=True) for d_, x in zip(den, p_l)]
            acc = [a + jnp.dot(x.astype(BF16), v_ref[pl.ds(start, n_win), cs], preferred_element_type=F32)
                   for a, x, cs in zip(acc, p_l, pairs)]
        for a, d_, cs in zip(acc, den, pairs):
            o = a / d_
            o_ref[qs, cs] = jnp.where(lo, o[:GRID_W], o[GRID_W:])

    @pl.when(j < ctx_steps)
    def _():
        for i in range(NA_ROWS_PER_STEP):
            attend(i, False)

    @pl.when(j >= ctx_steps)
    def _():
        for i in range(NA_ROWS_PER_STEP):
            attend(i, True)


def _na_attention(q, k, v, bias, n_ctx):
    batch, t, w = q.shape
    rows = (t - n_ctx) // GRID_W
    tq = GRID_W * NA_ROWS_PER_STEP
    assert n_ctx % tq == 0 and rows % NA_ROWS_PER_STEP == 0
    ctx_steps = n_ctx // tq
    bias = bias.reshape(NA_WIN_H, N_PAIRS, 2 * GRID_W, NA_WIN_H * GRID_W)

    def bias_spec(i):
        def idx(b, j):
            r = jnp.maximum(j - ctx_steps, 0) * NA_ROWS_PER_STEP + i
            row0 = jnp.clip(r - NA_WIN_H // 2, 0, rows - NA_WIN_H)
            return (r - row0, 0, 0, 0)
        return pl.BlockSpec((None, N_PAIRS, 2 * GRID_W, NA_WIN_H * GRID_W), idx)

    return pl.pallas_call(
        functools.partial(_na_kernel, n_ctx=n_ctx, rows=rows),
        grid=(batch, t // tq),
        in_specs=[
            pl.BlockSpec((None, tq, w), lambda b, j: (b, j, 0)),
            pl.BlockSpec((None, t, w), lambda b, j: (b, 0, 0)),
            pl.BlockSpec((None, t, w), lambda b, j: (b, 0, 0)),
        ] + [bias_spec(i) for i in range(NA_ROWS_PER_STEP)],
        out_specs=pl.BlockSpec((None, tq, w), lambda b, j: (b, j, 0)),
        out_shape=jax.ShapeDtypeStruct((batch, t, w), F32),
        compiler_params=_params(("parallel", "arbitrary")),
        name="na_attention",
    )(q, k, v, *([bias] * NA_ROWS_PER_STEP))


def _gelu_tanh(x):
    return x * (0.5 * (1.0 + jnp.tanh(0.7978845608028654 * (x + 0.044715 * (x * x * x)))))


def _rwkv_prep_kernel(stage_ref, rv_ref, dg_ref, q1_ref, y0_ref, m_ref, n0_ref):
    tm = TOKEN_TILE
    w = BRANCH_WIDTH
    ch = RWKV_CHUNK
    ci = lax.broadcasted_iota(jnp.int32, (ch, QUAD), 0)
    cj = lax.broadcasted_iota(jnp.int32, (ch, QUAD), 1) % HEAD_DIM
    eye_w = ci == cj
    eye_b = jnp.where(eye_w, 1.0, 0.0).astype(BF16)
    bi = lax.broadcasted_iota(jnp.int32, (QUAD, QUAD), 0) // HEAD_DIM
    bj = lax.broadcasted_iota(jnp.int32, (QUAD, QUAD), 1) // HEAD_DIM
    maskb = jnp.where(bi == bj, 1.0, 0.0).astype(BF16)

    strict = ((cj < ci), (cj > ci))
    incl = ((cj <= ci), (cj >= ci))

    def bd(x):
        return _block_diag(x.astype(BF16))

    def wide_dot(lhs, rhs):
        return jnp.dot(lhs.astype(BF16), rhs, preferred_element_type=F32)

    chains = [(d, c, qd) for c in range(tm // ch) for d in range(2) for qd in range(w // QUAD)]
    n_ch = len(chains)
    dirs = [d for d, _, _ in chains]

    def load(s):
        return [stage_ref[d, s, c * ch:(c + 1) * ch, qd * QUAD:(qd + 1) * QUAD] for d, c, qd in chains]

    ab, rb, bt, kt = load(0), load(1), load(2), load(3)
    lhs = [jnp.concatenate([a_, r_, eye_b], axis=0) for a_, r_ in zip(ab, rb)]
    g1 = [lax.dot_general(l_, _block_diag(x), _NT, preferred_element_type=F32) for l_, x in zip(lhs, bt)]
    g2 = [lax.dot_general(l_, _block_diag(x), _NT, preferred_element_type=F32) for l_, x in zip(lhs, kt)]
    diag_w = [jnp.where(eye_w, dg_ref[d, c, 0:1, qd * QUAD:(qd + 1) * QUAD], 0.0) for d, c, qd in chains]
    scale = wide_dot(jnp.concatenate(diag_w, axis=0), maskb)
    scale = [scale[s * ch:(s + 1) * ch] for s in range(n_ch)]
    apow = [jnp.where(strict[d], g[:ch], 0.0) for d, g in zip(dirs, g1)]
    h_rb = [jnp.where(incl[d], g[ch:2 * ch], 0.0) for d, g in zip(dirs, g1)]
    j_bh = [s_ * g[2 * ch:] for s_, g in zip(scale, g1)]
    a_ak = [jnp.where(strict[d], g[:ch], 0.0) for d, g in zip(dirs, g2)]
    tail = [jnp.concatenate([jnp.where(incl[d], g[ch:2 * ch], 0.0), s_ * g[2 * ch:]], axis=0)
            for d, s_, g in zip(dirs, scale, g2)]
    for step in range(6):
        grow = step < 5
        stacked = [jnp.concatenate(([p_] if grow else []) + [h_, j_], axis=0)
                   for p_, h_, j_ in zip(apow, h_rb, j_bh)]
        out = [wide_dot(l_, bd(p_)) for l_, p_ in zip(stacked, apow)]
        off = ch if grow else 0
        if grow:
            apow = [o[:ch] for o in out]
        h_rb = [x + o[off:off + ch] for x, o in zip(h_rb, out)]
        j_bh = [x + o[off + ch:] for x, o in zip(j_bh, out)]
    hj = [jnp.concatenate([h_, j_], axis=0).astype(BF16) for h_, j_ in zip(h_rb, j_bh)]
    x12 = [jnp.dot(l_, jnp.concatenate([_block_diag(a_), bd(k_)], axis=1), preferred_element_type=F32)
           for l_, a_, k_ in zip(hj, ab, a_ak)]
    vv = [rv_ref[c * ch:(c + 1) * ch, qd * QUAD:(qd + 1) * QUAD] for _, c, qd in chains]
    x3 = [wide_dot(x[:, QUAD:] + t_, _block_diag(v_)) for x, t_, v_ in zip(x12, tail, vv)]
    for s, (d, c, qd) in enumerate(chains):
        rs = slice(c * ch, (c + 1) * ch)
        cs = slice(qd * QUAD, (qd + 1) * QUAD)
        q1_ref[d, rs, cs] = (rb[s].astype(F32) + x12[s][:ch, :QUAD]).astype(BF16)
        y0_ref[d, rs, cs] = x3[s][:ch]
        m_ref[d, c, :, cs] = x12[s][ch:, :QUAD] + diag_w[s]
        n0_ref[d, c, :, cs] = x3[s][ch:]


def _rwkv_prep(stage, rv, dg):
    batch, _, _, t, w = stage.shape
    nt = t // TOKEN_TILE
    cpt = TOKEN_TILE // RWKV_CHUNK
    nc = t // RWKV_CHUNK
    dtok = pl.BlockSpec((None, 2, TOKEN_TILE, w), lambda b, j: (b, 0, j, 0))
    mat = pl.BlockSpec((None, 2, cpt, HEAD_DIM, w), lambda b, j: (b, 0, j, 0, 0))
    return pl.pallas_call(
        _rwkv_prep_kernel,
        grid=(batch, nt),
        in_specs=[
            pl.BlockSpec((None, 2, _N_STAGE, TOKEN_TILE, w), lambda b, j: (b, 0, 0, j, 0)),
            pl.BlockSpec((None, TOKEN_TILE, w), lambda b, j: (b, j, 0)),
            pl.BlockSpec((None, 2, cpt, 8, w), lambda b, j: (b, 0, j, 0, 0)),
        ],
        out_specs=[dtok, dtok, mat, mat],
        out_shape=[
            jax.ShapeDtypeStruct((batch, 2, t, w), BF16),
            jax.ShapeDtypeStruct((batch, 2, t, w), F32),
            jax.ShapeDtypeStruct((batch, 2, nc, HEAD_DIM, w), F32),
            jax.ShapeDtypeStruct((batch, 2, nc, HEAD_DIM, w), F32),
        ],
        compiler_params=_params(("parallel", "parallel")),
        name="rwkv_prep",
    )(stage, rv, dg)


def _rwkv_chain_kernel(q1f_ref, mf_ref, n0f_ref, q1b_ref, mb_ref, n0b_ref, yf_ref, yb_ref, s_ref):
    @pl.when(pl.program_id(0) == 0)
    def _():
        s_ref[...] = jnp.zeros_like(s_ref)

    batch = q1f_ref.shape[0]
    ch = RWKV_CHUNK
    dirs = ((q1f_ref, mf_ref, n0f_ref, yf_ref), (q1b_ref, mb_ref, n0b_ref, yb_ref))

    def split(x):
        hi = x.astype(BF16)
        return hi, (x - hi.astype(F32)).astype(BF16)

    chains = [(d, b, slice(qd * QUAD, (qd + 1) * QUAD))
              for d in range(2) for b in range(batch) for qd in range(BRANCH_WIDTH // QUAD)]
    outs = []
    for d, b, cs in chains:
        q1_ref, m_ref, _, _ = dirs[d]
        s_hi, s_lo = split(s_ref[d, b, :, cs])
        m_hi, m_lo = split(m_ref[b, :, cs])
        lhs = jnp.concatenate([q1_ref[b, :, cs], m_hi, m_lo], axis=0)
        o = jnp.dot(lhs, _block_diag(s_hi), preferred_element_type=F32)
        o_lo = jnp.dot(m_hi, _block_diag(s_lo), preferred_element_type=F32)
        outs.append((o[:ch], o[ch:2 * ch] + o[2 * ch:] + o_lo))
    for (d, b, cs), (y, s_new) in zip(chains, outs):
        _, _, n0_ref, y_ref = dirs[d]
        y_ref[b, :, cs] = y
        s_ref[d, b, :, cs] = s_new + n0_ref[b, :, cs]


def _rwkv_chain(q1, m, n0, n_ctx):
    batch, _, t, w = q1.shape
    nc = t // RWKV_CHUNK
    ncc = n_ctx // RWKV_CHUNK

    def chunk(d, s):
        return s if d == 0 else jnp.where(s < ncc, ncc - 1 - s, nc - 1 - (s - ncc))

    in_specs = []
    for d in range(2):
        tok = pl.BlockSpec((batch, None, RWKV_CHUNK, w), lambda s, d=d: (0, d, chunk(d, s), 0))
        mat = pl.BlockSpec((batch, None, None, HEAD_DIM, w), lambda s, d=d: (0, d, chunk(d, s), 0, 0))
        in_specs += [tok, mat, mat]
    out_specs = [pl.BlockSpec((batch, RWKV_CHUNK, w), lambda s, d=d: (0, chunk(d, s), 0)) for d in range(2)]
    return pl.pallas_call(
        _rwkv_chain_kernel,
        grid=(nc,),
        in_specs=in_specs,
        out_specs=out_specs,
        out_shape=[jax.ShapeDtypeStruct((batch, t, w), F32)] * 2,
        scratch_shapes=[pltpu.VMEM((2, batch, HEAD_DIM, w), F32)],
        compiler_params=_params(("arbitrary",)),
        name="rwkv_chain",
    )(q1, m, n0, q1, m, n0)


_FF_CHUNK = 256


def _post_kernel(x_ref, mod_ref, gates_ref, ona_ref, yf_ref, yb_ref, y0_ref, g_ref, bv_ref, psgu_ref,
                 gng_ref, gnb_ref, slng_ref, slnb_ref, sw_ref, sb_ref,
                 wb_ref, wo_ref, ln1g_ref, ln1b_ref, wgu_ref, wd_ref, ln2g_ref, ln2b_ref,
                 o_ref, osgu_ref, act_ref, *, alpha):
    w = BRANCH_WIDTH
    d = D_MODEL

    gl = _gelu_tanh(psgu_ref[...].astype(F32))
    u = gl[:, :w]
    v = _layer_norm(gl[:, w:], slng_ref[...], slnb_ref[...], LN_EPS).astype(BF16)
    lo = lax.broadcasted_iota(jnp.int32, (SGU_CHUNK, LANES), 1) < HEAD_DIM
    for ck in range(TOKEN_TILE // SGU_CHUNK):
        rs = slice(ck * SGU_CHUNK, (ck + 1) * SGU_CHUNK)
        for p in range(N_PAIRS):
            cs = slice(p * LANES, (p + 1) * LANES)
            vp = v[rs, cs]
            r0 = jnp.dot(sw_ref[2 * p], vp, preferred_element_type=F32)
            r1 = jnp.dot(sw_ref[2 * p + 1], vp, preferred_element_type=F32)
            osgu_ref[rs, cs] = (u[rs, cs] * (jnp.where(lo, r0, r1) + sb_ref[:, cs])).astype(BF16)

    y = (yf_ref[...] + y0_ref[0]) + (yb_ref[...] + y0_ref[1])
    mu = _head_sum(y) * (1.0 / HEAD_DIM)
    yc = y - mu
    var = _head_sum(yc * yc) * (1.0 / HEAD_DIM)
    y = yc * lax.rsqrt(var + RWKV_GN_EPS) * gng_ref[...] + gnb_ref[...]
    o_rwkv = (y + bv_ref[...]) * g_ref[...]

    acc = _sigmoid(gates_ref[:, 0:d].astype(F32)) * _dot(ona_ref[...], wb_ref[0])
    acc = acc + _sigmoid(gates_ref[:, d:2 * d].astype(F32)) * _dot(o_rwkv, wb_ref[1])
    acc = acc + _sigmoid(gates_ref[:, 2 * d:3 * d].astype(F32)) * jnp.dot(
        osgu_ref[...], wb_ref[2], preferred_element_type=F32)
    z = _dot(acc, wo_ref[...])
    x1 = _layer_norm(alpha * x_ref[...] + mod_ref[2:3, :] * z, ln1g_ref[...], ln1b_ref[...], LN_EPS)

    h = (x1 * (1.0 + mod_ref[4:5, :]) + mod_ref[3:4, :]).astype(BF16)
    for c0 in range(0, D_FF, _FF_CHUNK):
        gate = jnp.dot(h, wgu_ref[:, c0:c0 + _FF_CHUNK], preferred_element_type=F32)
        up = jnp.dot(h, wgu_ref[:, D_FF + c0:D_FF + c0 + _FF_CHUNK], preferred_element_type=F32)
        act_ref[:, c0:c0 + _FF_CHUNK] = (gate * _sigmoid(gate) * up).astype(BF16)
    z = jnp.dot(act_ref[...], wd_ref[...], preferred_element_type=F32)
    o_ref[...] = _layer_norm(alpha * x1 + mod_ref[5:6, :] * z, ln2g_ref[...], ln2b_ref[...], LN_EPS)


def _post(x, mods, gates, o_na, y_f, y_b, y0, g, bv, p_sgu, vecs, sgu_w, sgu_b_full, w_branch, w_out, w_gu, w_down,
          layer, n_ctx_tiles, alpha, skip_tiles):
    batch, t, d = x.shape
    nt = t // TOKEN_TILE - skip_tiles
    w = BRANCH_WIDTH
    row = _mod_row(n_ctx_tiles, batch)
    c2 = lambda b, j: (0, 0)
    tok = lambda n: pl.BlockSpec((None, TOKEN_TILE, n), lambda b, j: (b, j + skip_tiles, 0))
    vw = pl.BlockSpec((1, w), c2)
    vd = pl.BlockSpec((1, d), c2)
    once = dict(pipeline_mode=pl.Buffered(1))
    return pl.pallas_call(
        functools.partial(_post_kernel, alpha=alpha),
        grid=(batch, nt),
        in_specs=[
            tok(d),
            pl.BlockSpec((None, None, 6, d), lambda b, j: (layer, row(b, j + skip_tiles), 0, 0)),
            tok(3 * d), tok(w), tok(w), tok(w),
            pl.BlockSpec((None, 2, TOKEN_TILE, w), lambda b, j: (b, 0, j + skip_tiles, 0)),
            tok(w), tok(w), tok(2 * w),
            vw, vw, vw, vw,
            pl.BlockSpec((None, SGU_GROUPS, SGU_CHUNK, SGU_CHUNK), lambda b, j: (layer, 0, 0, 0)),
            pl.BlockSpec((SGU_CHUNK, w), c2),
            pl.BlockSpec((None, 3, w, d), lambda b, j: (layer, 0, 0, 0), **once),
            pl.BlockSpec((None, d, d), lambda b, j: (layer, 0, 0), **once),
            vd, vd,
            pl.BlockSpec((None, d, 2 * D_FF), lambda b, j: (layer, 0, 0), **once),
            pl.BlockSpec((None, D_FF, d), lambda b, j: (layer, 0, 0), **once),
            vd, vd,
        ],
        out_specs=pl.BlockSpec((None, TOKEN_TILE, d), lambda b, j: (b, j, 0)),
        out_shape=jax.ShapeDtypeStruct((batch, nt * TOKEN_TILE, d), F32),
        scratch_shapes=[pltpu.VMEM((TOKEN_TILE, w), BF16), pltpu.VMEM((TOKEN_TILE, D_FF), BF16)],
        compiler_params=_params(("parallel", "parallel")),
        name="post",
    )(x, mods, gates, o_na, y_f, y_b, y0, g, bv, p_sgu, vecs["gn_g"], vecs["gn_b"], vecs["sgu_ln_g"],
      vecs["sgu_ln_b"], sgu_w, sgu_b_full, w_branch, w_out, vecs["ln1_g"], vecs["ln1_b"], w_gu, w_down,
      vecs["ln2_g"], vecs["ln2_b"])


def kernel(x, c, ctx, c_ctx, w_ada, b_ada, w_in, na_rpb, rwkv_mu_prev, rwkv_mu_next, rwkv_w0, rwkv_w2, rwkv_a0,
           rwkv_a2, rwkv_g2, rwkv_k_k, rwkv_k_a, rwkv_r_k, rwkv_gn_g, rwkv_gn_b, sgu_ln_g, sgu_ln_b, sgu_w, sgu_b,
           w_branch, w_out, ln1_g, ln1_b, ln2_g, ln2_b, ffn_w_gu, ffn_w_down):
    batch, seq, d = x.shape
    n_ctx = ctx.shape[1]
    depth = w_ada.shape[0]
    assert d == D_MODEL and batch <= 7
    assert n_ctx % TOKEN_TILE == 0 and seq % TOKEN_TILE == 0
    assert seq // GRID_W >= NA_WIN_H
    n_ctx_tiles = n_ctx // TOKEN_TILE
    alpha = float((2 * depth) ** 0.25)

    cond = jnp.zeros((8, d), F32).at[:batch].set(c).at[batch].set(c_ctx)
    mods = jnp.transpose(_ada_all(cond, w_ada, b_ada), (0, 2, 1, 3))

    xs = jnp.concatenate([ctx, x], axis=1)
    w_in_b = w_in.astype(BF16)
    w_branch_b = w_branch.astype(BF16)
    w_out_b = w_out.astype(BF16)
    w_gu_b = ffn_w_gu.astype(BF16)
    w_down_b = ffn_w_down.astype(BF16)
    sgu_w_b = sgu_w.astype(BF16)
    row = lambda a: a.reshape(1, -1)

    for l in range(depth):
        last = l == depth - 1
        prm = dict(
            mu_prev=row(rwkv_mu_prev[l]), mu_next=row(rwkv_mu_next[l]),
            w0=rwkv_w0[l], w2=rwkv_w2[l].reshape(2 * DECAY_LORA, BRANCH_WIDTH).astype(BF16),
            a0=rwkv_a0[l], a2=rwkv_a2[l].reshape(2 * DECAY_LORA, BRANCH_WIDTH).astype(BF16),
            g2=rwkv_g2[l].astype(BF16), k_k=row(rwkv_k_k[l]), k_a=row(rwkv_k_a[l]), r_k=row(rwkv_r_k[l]))
        gates, q, k, v, p_sgu, stage, rv, dg, g, bv = _inproj(xs, mods, w_in_b, prm, l, n_ctx_tiles)
        o_na = _na_attention(q, k, v, _na_bias_table(na_rpb[l]), n_ctx)
        q1, y0, m, n0 = _rwkv_prep(stage, rv, dg)
        y_f, y_b = _rwkv_chain(q1, m, n0, n_ctx)
        b_full = jnp.repeat(jnp.transpose(sgu_b[l]), BRANCH_WIDTH // SGU_GROUPS, axis=1)
        vecs = dict(gn_g=row(rwkv_gn_g[l]), gn_b=row(rwkv_gn_b[l]), sgu_ln_g=row(sgu_ln_g[l]),
                    sgu_ln_b=row(sgu_ln_b[l]), ln1_g=row(ln1_g[l]), ln1_b=row(ln1_b[l]), ln2_g=row(ln2_g[l]),
                    ln2_b=row(ln2_b[l]))
        xs = _post(xs, mods, gates, o_na, y_f, y_b, y0, g, bv, p_sgu, vecs, sgu_w_b, b_full, w_branch_b, w_out_b,
                   w_gu_b, w_down_b, l, n_ctx_tiles, alpha, n_ctx_tiles if last else 0)
    return xs
```

```python
import functools

import jax
import jax.numpy as jnp
import numpy as np
from jax import lax
from jax.experimental import pallas as pl
from jax.experimental.pallas import tpu as pltpu

F32 = jnp.float32
BF16 = jnp.bfloat16

D_MODEL = 1024
GRID_W = 64
NA_HEADS = 8
HEAD_DIM = 64
NA_WIN_H = 8
NA_WIN_W = 16
BRANCH_WIDTH = 512
DECAY_LORA = 64
GATE_LORA = 128
RWKV_IN = 3 * BRANCH_WIDTH + 4 * DECAY_LORA + GATE_LORA
RWKV_GN_EPS = 64e-5
SGU_CHUNK = 128
SGU_GROUPS = 8
D_FF = 2816
LN_EPS = 1e-5
MASK_VALUE = -1e30

LANES = 128
TOKEN_TILE = 256
RWKV_CHUNK = 64
N_PAIRS = NA_HEADS // 2
QUAD = 4 * HEAD_DIM
HIGHEST = lax.Precision.HIGHEST
VMEM_LIMIT = 56 * 1024 * 1024

_NT = (((1,), (1,)), ((), ()))


def _params(sem, vmem=VMEM_LIMIT):
    return pltpu.CompilerParams(dimension_semantics=sem, vmem_limit_bytes=vmem)


def _dot(a, b):
    return jnp.dot(a.astype(BF16), b.astype(BF16), preferred_element_type=F32)


def _layer_norm(x, g, b, eps):
    mu = jnp.mean(x, axis=-1, keepdims=True)
    xc = x - mu
    var = jnp.mean(xc * xc, axis=-1, keepdims=True)
    return xc * lax.rsqrt(var + eps) * g + b


def _sigmoid(x):
    return 0.5 * jnp.tanh(0.5 * x) + 0.5


def _head_sum(x):
    rows = x.shape[0]
    lo = lax.broadcasted_iota(jnp.int32, (rows, LANES), 1) < HEAD_DIM
    out = []
    for p in range(x.shape[1] // LANES):
        xp = x[:, p * LANES:(p + 1) * LANES]
        s_lo = jnp.sum(jnp.where(lo, xp, 0.0), axis=-1, keepdims=True)
        s_hi = jnp.sum(jnp.where(lo, 0.0, xp), axis=-1, keepdims=True)
        out.append(jnp.where(lo, s_lo, s_hi))
    return jnp.concatenate(out, axis=1)


def _block_diag(xb):
    rows = xb.shape[0]
    lane = lax.broadcasted_iota(jnp.int32, (rows, LANES), 1)
    lo_b = jnp.where(lane < HEAD_DIM, 1.0, 0.0).astype(BF16)
    hi_b = jnp.where(lane >= HEAD_DIM, 1.0, 0.0).astype(BF16)
    zero_b = jnp.zeros((rows, LANES), BF16)
    t0, t1 = xb[:, :LANES], xb[:, LANES:]
    return jnp.concatenate([
        jnp.concatenate([t0 * lo_b, zero_b], axis=1), jnp.concatenate([t0 * hi_b, zero_b], axis=1),
        jnp.concatenate([zero_b, t1 * lo_b], axis=1), jnp.concatenate([zero_b, t1 * hi_b], axis=1)], axis=0)


def _ada_kernel(c_ref, w_ref, b_ref, o_ref):
    c = c_ref[...]
    o_ref[...] = _dot(c * _sigmoid(c), w_ref[...]) + b_ref[...]


def _ada_all(cond, w_ada, b_ada):
    depth, d, n = w_ada.shape
    n_mod = n // d
    return pl.pallas_call(
        _ada_kernel,
        grid=(depth, n_mod),
        in_specs=[
            pl.BlockSpec((8, d), lambda l, i: (0, 0)),
            pl.BlockSpec((None, d, d), lambda l, i: (l, 0, i)),
            pl.BlockSpec((None, 1, d), lambda l, i: (l, 0, i)),
        ],
        out_specs=pl.BlockSpec((None, None, 8, d), lambda l, i: (l, i, 0, 0)),
        out_shape=jax.ShapeDtypeStruct((depth, n_mod, 8, d), F32),
        compiler_params=_params(("parallel", "parallel")),
        name="adaln",
    )(cond, w_ada, b_ada.reshape(depth, 1, n))


_IN_SPLITS = (0, 3 * D_MODEL, 3 * D_MODEL + 512, 3 * D_MODEL + 1024, 3 * D_MODEL + 1536,
              3 * D_MODEL + 1536 + RWKV_IN, 3 * D_MODEL + 1536 + RWKV_IN + 2 * BRANCH_WIDTH)
_IN_COL_CHUNK = 512
_HALO = 8
_N_STAGE = 4


def _inproj_kernel(x_ref, xp_ref, xn_ref, mod_ref, w_ref, mup_ref, mun_ref, w0_ref, w2_ref, a0_ref, a2_ref,
                   g2_ref, kk_ref, ka_ref, rk_ref,
                   gates_ref, q_ref, k_ref, v_ref, ps_ref, stage_ref, rv_ref, dg_ref, g_ref, bv_ref, p_scr,
                   *, n_ctx_tiles, n_tiles):
    j = pl.program_id(1)
    tm = TOKEN_TILE
    w = BRANCH_WIDTH
    ch = RWKV_CHUNK
    shift = mod_ref[0:1, :]
    scale1 = 1.0 + mod_ref[1:2, :]
    h = (x_ref[...] * scale1 + shift).astype(BF16)

    h_ext = jnp.concatenate([(xp_ref[...] * scale1 + shift).astype(BF16), h,
                             (xn_ref[...] * scale1 + shift).astype(BF16)], axis=0)
    lo_r, hi_r = _IN_SPLITS[4], _IN_SPLITS[5]
    for c0 in range(lo_r, hi_r, _IN_COL_CHUNK):
        c1 = min(c0 + _IN_COL_CHUNK, hi_r)
        p_scr[:, c0 - lo_r:c1 - lo_r] = jnp.dot(h_ext, w_ref[:, c0:c1], preferred_element_type=F32)

    def project():
        outs = ((0, gates_ref), (1, q_ref), (2, k_ref), (3, v_ref), (5, ps_ref))
        for idx, o_ref in outs:
            lo, hi = _IN_SPLITS[idx], _IN_SPLITS[idx + 1]
            scale = HEAD_DIM ** -0.5 if idx == 1 else None
            for c0 in range(lo, hi, _IN_COL_CHUNK):
                c1 = min(c0 + _IN_COL_CHUNK, hi)
                r = jnp.dot(h, w_ref[:, c0:c1], preferred_element_type=F32)
                if scale is not None:
                    r = r * scale
                o_ref[:, c0 - lo:c1 - lo] = r.astype(o_ref.dtype)
                yield

    lo_dir = lax.broadcasted_iota(jnp.int32, (tm, LANES), 1) < DECAY_LORA
    ti = lax.broadcasted_iota(jnp.int32, (tm, tm), 0)
    si = lax.broadcasted_iota(jnp.int32, (tm, tm), 1)
    same_chunk = (ti // ch) == (si // ch)

    def stage_tile():
        p = p_scr[_HALO:_HALO + tm, :]
        first = (j == 0) | (j == n_ctx_tiles)
        last = (j == n_ctx_tiles - 1) | (j == n_tiles - 1)
        prev_row = jnp.where(first, 0.0, p_scr[_HALO - 1:_HALO, :])
        next_row = jnp.where(last, 0.0, p_scr[_HALO + tm:_HALO + tm + 1, :])
        row8 = lax.broadcasted_iota(jnp.int32, (8, 1), 0)
        down = pltpu.roll(p, 1, 0)
        up = pltpu.roll(p, tm - 1, 0)
        p_prev = jnp.concatenate([jnp.where(row8 == 0, prev_row, down[:8]), down[8:]], axis=0)
        p_next = jnp.concatenate([up[:tm - 8], jnp.where(row8 == 7, next_row, up[tm - 8:])], axis=0)
        ps = (1.0 - mup_ref[...] - mun_ref[...]) * p + mup_ref[...] * p_prev + mun_ref[...] * p_next
        yield
        r = ps[:, 0:w]
        k = ps[:, w:2 * w]
        v = ps[:, 2 * w:3 * w]
        wc = jnp.tanh(ps[:, 3 * w:3 * w + LANES])
        ac = ps[:, 3 * w + LANES:3 * w + 2 * LANES]
        gc = ps[:, 3 * w + 2 * LANES:3 * w + 3 * LANES]
        g_ref[...] = _dot(_sigmoid(gc), g2_ref[...])
        rv_ref[...] = v.astype(BF16)
        yield
        kk = k * kk_ref[...]
        kk = kk * lax.rsqrt(_head_sum(kk * kk) + 1e-12)
        yield
        kd_sum = jnp.zeros((tm, w), F32)
        for d in range(2):
            md = lo_dir if d == 0 else ~lo_dir
            z = w0_ref[d:d + 1, :] + _dot(jnp.where(md, wc, 0.0), w2_ref[...])
            lw = -float(np.exp(-0.5)) * _sigmoid(z)
            a = _sigmoid(a0_ref[d:d + 1, :] + _dot(jnp.where(md, ac, 0.0), a2_ref[...]))
            yield
            kd = k * (1.0 + (a - 1.0) * ka_ref[...])
            kd_sum = kd_sum + kd
            beta = kk * a
            before = (si <= ti) if d == 0 else (si >= ti)
            tri = jnp.where(same_chunk & before, 1.0, 0.0).astype(BF16)
            lw_hi = lw.astype(BF16)
            rem = lw - lw_hi.astype(F32)
            lw_mid = rem.astype(BF16)
            lw_lo = (rem - lw_mid.astype(F32)).astype(BF16)
            parts = jnp.dot(tri, jnp.concatenate([lw_hi, lw_mid, lw_lo], axis=1), preferred_element_type=F32)
            cum = (parts[:, 2 * w:] + parts[:, w:2 * w]) + parts[:, :w]
            yield
            e_neg = jnp.exp(-cum)
            stage_ref[d, 0] = (-kk * jnp.exp(cum - lw)).astype(BF16)
            stage_ref[d, 1] = (r * jnp.exp(cum)).astype(BF16)
            yield
            stage_ref[d, 2] = (beta * e_neg).astype(BF16)
            stage_ref[d, 3] = (kd * e_neg).astype(BF16)
            last_step = ch - 1 if d == 0 else 0
            for c in range(tm // ch):
                tot = cum[c * ch + last_step:c * ch + last_step + 1]
                dg_ref[d, c] = jnp.broadcast_to(jnp.exp(tot), (8, w))
            yield
        bv_ref[...] = _head_sum(r * kd_sum * rk_ref[...]) * v
        yield

    gens = [project(), stage_tile()]
    while gens:
        for gen in list(gens):
            if next(gen, StopIteration) is StopIteration:
                gens.remove(gen)


def _mod_row(n_ctx_tiles, batch):
    return lambda b, j: jnp.where(j < n_ctx_tiles, batch, b)


def _inproj(x, mods, w_in, prm, layer, n_ctx_tiles):
    batch, t, d = x.shape
    nt = t // TOKEN_TILE
    cpt = TOKEN_TILE // RWKV_CHUNK
    nc = t // RWKV_CHUNK
    d_in = w_in.shape[-1]
    w = BRANCH_WIDTH
    halo = TOKEN_TILE // _HALO
    row = _mod_row(n_ctx_tiles, batch)
    c2 = lambda b, j: (0, 0)
    vec = lambda n: pl.BlockSpec((1, n), c2)
    tok = lambda n: pl.BlockSpec((None, TOKEN_TILE, n), lambda b, j: (b, j, 0))
    widths = [_IN_SPLITS[i + 1] - _IN_SPLITS[i] for i in (0, 1, 2, 3, 5)]
    return pl.pallas_call(
        functools.partial(_inproj_kernel, n_ctx_tiles=n_ctx_tiles, n_tiles=nt),
        grid=(batch, nt),
        in_specs=[
            tok(d),
            pl.BlockSpec((None, _HALO, d), lambda b, j: (b, jnp.maximum(j * halo - 1, 0), 0)),
            pl.BlockSpec((None, _HALO, d), lambda b, j: (b, jnp.minimum((j + 1) * halo, t // _HALO - 1), 0)),
            pl.BlockSpec((None, None, 6, d), lambda b, j: (layer, row(b, j), 0, 0)),
            pl.BlockSpec((None, d, d_in), lambda b, j: (layer, 0, 0), pipeline_mode=pl.Buffered(1)),
            vec(RWKV_IN), vec(RWKV_IN),
            pl.BlockSpec((2, w), c2), pl.BlockSpec((2 * DECAY_LORA, w), c2),
            pl.BlockSpec((2, w), c2), pl.BlockSpec((2 * DECAY_LORA, w), c2),
            pl.BlockSpec((GATE_LORA, w), c2),
            vec(w), vec(w), vec(w),
        ],
        out_specs=[tok(n) for n in widths] + [
            pl.BlockSpec((None, 2, _N_STAGE, TOKEN_TILE, w), lambda b, j: (b, 0, 0, j, 0)),
            tok(w),
            pl.BlockSpec((None, 2, cpt, 8, w), lambda b, j: (b, 0, j, 0, 0)),
            tok(w), tok(w)],
        out_shape=[jax.ShapeDtypeStruct((batch, t, n), BF16) for n in widths] + [
            jax.ShapeDtypeStruct((batch, 2, _N_STAGE, t, w), BF16),
            jax.ShapeDtypeStruct((batch, t, w), BF16),
            jax.ShapeDtypeStruct((batch, 2, nc, 8, w), F32),
            jax.ShapeDtypeStruct((batch, t, w), F32),
            jax.ShapeDtypeStruct((batch, t, w), F32)],
        scratch_shapes=[pltpu.VMEM((TOKEN_TILE + 2 * _HALO, RWKV_IN), F32)],
        compiler_params=_params(("parallel", "parallel")),
        name="inproj",
    )(x, x, x, mods, w_in, prm["mu_prev"], prm["mu_next"], prm["w0"], prm["w2"], prm["a0"], prm["a2"],
      prm["g2"], prm["k_k"], prm["k_a"], prm["r_k"])


def _na_bias_table(rpb):
    cols = np.arange(GRID_W)
    col0 = np.clip(cols - NA_WIN_W // 2, 0, GRID_W - NA_WIN_W)
    kc = np.arange(GRID_W)
    in_win = (kc[None, :] >= col0[:, None]) & (kc[None, :] < col0[:, None] + NA_WIN_W)
    dcol = kc[None, :] - cols[:, None] + NA_WIN_W - 1
    n_dcol = 2 * NA_WIN_W - 1
    onehot = ((dcol[None] == np.arange(n_dcol)[:, None, None]) & in_win[None]).astype(np.float32)
    toe = jnp.einsum("hrd,dck->hrck", rpb, jnp.asarray(onehot), precision=HIGHEST)
    toe = toe + jnp.asarray(np.where(in_win, 0.0, MASK_VALUE).astype(np.float32))
    two = jnp.concatenate([toe[:, :-1], toe[:, 1:]], axis=-1)
    two = two.reshape(N_PAIRS, 2, 2 * NA_WIN_H - 2, GRID_W, 2 * GRID_W)
    return jnp.transpose(two, (0, 2, 1, 3, 4)).reshape(N_PAIRS, 2 * NA_WIN_H - 2, 2 * GRID_W, 2 * GRID_W)


NA_ROWS_PER_STEP = 4


def _na_kernel(q_ref, k_ref, v_ref, bias_ref, o_ref, *, n_ctx, rows):
    j = pl.program_id(1)
    ctx_steps = n_ctx // (GRID_W * NA_ROWS_PER_STEP)
    lo2 = lax.broadcasted_iota(jnp.int32, (2 * GRID_W, LANES), 1) < HEAD_DIM
    top = lax.broadcasted_iota(jnp.int32, (2 * GRID_W, LANES), 0) < GRID_W
    own_head = lo2 == top
    lo = lax.broadcasted_iota(jnp.int32, (GRID_W, LANES), 1) < HEAD_DIM
    n_win = NA_WIN_H * GRID_W
    pairs = [slice(p * LANES, (p + 1) * LANES) for p in range(N_PAIRS)]

    def attend(i, local):
        qs = slice(i * GRID_W, (i + 1) * GRID_W)
        if local:
            r = (j - ctx_steps) * NA_ROWS_PER_STEP + i
            row0 = jnp.clip(r - NA_WIN_H // 2, 0, rows - NA_WIN_H)
            start = pl.multiple_of(n_ctx + row0 * GRID_W, GRID_W)
            off0 = NA_WIN_H - 1 - (r - row0)
        q2 = []
        for cs in pairs:
            qp = q_ref[qs, cs].astype(F32)
            q2.append(jnp.where(own_head, jnp.concatenate([qp, qp], axis=0), 0.0).astype(BF16))
        s_c = [lax.dot_general(q_, k_ref[0:n_ctx, cs], _NT, preferred_element_type=F32)
               for q_, cs in zip(q2, pairs)]
        mx = [jnp.max(s, axis=-1, keepdims=True) for s in s_c]
        if local:
            s_l = [lax.dot_general(q_, k_ref[pl.ds(start, n_win), cs], _NT, preferred_element_type=F32)
                   + jnp.concatenate([bias_ref[p, off0 + 2 * m] for m in range(NA_WIN_H // 2)], axis=1)
                   for p, (q_, cs) in enumerate(zip(q2, pairs))]
            mx = [jnp.maximum(m, jnp.max(s, axis=-1, keepdims=True)) for m, s in zip(mx, s_l)]
            p_l = [jnp.exp(s - m) for s, m in zip(s_l, mx)]
        p_c = [jnp.exp(s - m) for s, m in zip(s_c, mx)]
        den = [jnp.sum(x, axis=-1, keepdims=True) for x in p_c]
        acc = [jnp.dot(x.astype(BF16), v_ref[0:n_ctx, cs], preferred_element_type=F32)
               for x, cs in zip(p_c, pairs)]
        if local:
            den = [d_ + jnp.sum(x, axis=-1, keepdims=True) for d_, x in zip(den, p_l)]
            acc = [a + jnp.dot(x.astype(BF16), v_ref[pl.ds(start, n_win), cs], preferred_element_type=F32)
                   for a, x, cs in zip(acc, p_l, pairs)]
        for a, d_, cs in zip(acc, den, pairs):
            o = a / d_
            o_ref[qs, cs] = jnp.where(lo, o[:GRID_W], o[GRID_W:])

    @pl.when(j < ctx_steps)
    def _():
        for i in range(NA_ROWS_PER_STEP):
            attend(i, False)

    @pl.when(j >= ctx_steps)
    def _():
        for i in range(NA_ROWS_PER_STEP):
            attend(i, True)


def _na_attention(q, k, v, bias, n_ctx):
    batch, t, w = q.shape
    rows = (t - n_ctx) // GRID_W
    tq = GRID_W * NA_ROWS_PER_STEP
    assert n_ctx % tq == 0 and rows % NA_ROWS_PER_STEP == 0
    return pl.pallas_call(
        functools.partial(_na_kernel, n_ctx=n_ctx, rows=rows),
        grid=(batch, t // tq),
        in_specs=[
            pl.BlockSpec((None, tq, w), lambda b, j: (b, j, 0)),
            pl.BlockSpec((None, t, w), lambda b, j: (b, 0, 0)),
            pl.BlockSpec((None, t, w), lambda b, j: (b, 0, 0)),
            pl.BlockSpec(bias.shape, lambda b, j: (0, 0, 0, 0)),
        ],
        out_specs=pl.BlockSpec((None, tq, w), lambda b, j: (b, j, 0)),
        out_shape=jax.ShapeDtypeStruct((batch, t, w), F32),
        compiler_params=_params(("parallel", "arbitrary")),
        name="na_attention",
    )(q, k, v, bias)


def _gelu_tanh(x):
    return x * (0.5 * (1.0 + jnp.tanh(0.7978845608028654 * (x + 0.044715 * (x * x * x)))))


def _rwkv_prep_kernel(stage_ref, rv_ref, dg_ref, q1_ref, y0_ref, m_ref, n0_ref):
    tm = TOKEN_TILE
    w = BRANCH_WIDTH
    ch = RWKV_CHUNK
    ci = lax.broadcasted_iota(jnp.int32, (ch, QUAD), 0)
    cj = lax.broadcasted_iota(jnp.int32, (ch, QUAD), 1) % HEAD_DIM
    eye_w = ci == cj
    eye_b = jnp.where(eye_w, 1.0, 0.0).astype(BF16)
    bi = lax.broadcasted_iota(jnp.int32, (QUAD, QUAD), 0) // HEAD_DIM
    bj = lax.broadcasted_iota(jnp.int32, (QUAD, QUAD), 1) // HEAD_DIM
    maskb = jnp.where(bi == bj, 1.0, 0.0).astype(BF16)

    strict = ((cj < ci), (cj > ci))
    incl = ((cj <= ci), (cj >= ci))

    def bd(x):
        return _block_diag(x.astype(BF16))

    def wide_dot(lhs, rhs):
        return jnp.dot(lhs.astype(BF16), rhs, preferred_element_type=F32)

    chains = [(d, c, qd) for c in range(tm // ch) for d in range(2) for qd in range(w // QUAD)]
    n_ch = len(chains)
    dirs = [d for d, _, _ in chains]

    def load(s):
        return [stage_ref[d, s, c * ch:(c + 1) * ch, qd * QUAD:(qd + 1) * QUAD] for d, c, qd in chains]

    ab, rb, bt, kt = load(0), load(1), load(2), load(3)
    lhs = [jnp.concatenate([a_, r_, eye_b], axis=0) for a_, r_ in zip(ab, rb)]
    g1 = [lax.dot_general(l_, _block_diag(x), _NT, preferred_element_type=F32) for l_, x in zip(lhs, bt)]
    g2 = [lax.dot_general(l_, _block_diag(x), _NT, preferred_element_type=F32) for l_, x in zip(lhs, kt)]
    diag_w = [jnp.where(eye_w, dg_ref[d, c, 0:1, qd * QUAD:(qd + 1) * QUAD], 0.0) for d, c, qd in chains]
    scale = wide_dot(jnp.concatenate(diag_w, axis=0), maskb)
    scale = [scale[s * ch:(s + 1) * ch] for s in range(n_ch)]
    apow = [jnp.where(strict[d], g[:ch], 0.0) for d, g in zip(dirs, g1)]
    h_rb = [jnp.where(incl[d], g[ch:2 * ch], 0.0) for d, g in zip(dirs, g1)]
    j_bh = [s_ * g[2 * ch:] for s_, g in zip(scale, g1)]
    a_ak = [jnp.where(strict[d], g[:ch], 0.0) for d, g in zip(dirs, g2)]
    tail = [jnp.concatenate([jnp.where(incl[d], g[ch:2 * ch], 0.0), s_ * g[2 * ch:]], axis=0)
            for d, s_, g in zip(dirs, scale, g2)]
    for step in range(6):
        grow = step < 5
        stacked = [jnp.concatenate(([p_] if grow else []) + [h_, j_], axis=0)
                   for p_, h_, j_ in zip(apow, h_rb, j_bh)]
        out = [wide_dot(l_, bd(p_)) for l_, p_ in zip(stacked, apow)]
        off = ch if grow else 0
        if grow:
            apow = [o[:ch] for o in out]
        h_rb = [x + o[off:off + ch] for x, o in zip(h_rb, out)]
        j_bh = [x + o[off + ch:] for x, o in zip(j_bh, out)]
    hj = [jnp.concatenate([h_, j_], axis=0).astype(BF16) for h_, j_ in zip(h_rb, j_bh)]
    x12 = [jnp.dot(l_, jnp.concatenate([_block_diag(a_), bd(k_)], axis=1), preferred_element_type=F32)
           for l_, a_, k_ in zip(hj, ab, a_ak)]
    vv = [rv_ref[c * ch:(c + 1) * ch, qd * QUAD:(qd + 1) * QUAD] for _, c, qd in chains]
    x3 = [wide_dot(x[:, QUAD:] + t_, _block_diag(v_)) for x, t_, v_ in zip(x12, tail, vv)]
    for s, (d, c, qd) in enumerate(chains):
        rs = slice(c * ch, (c + 1) * ch)
        cs = slice(qd * QUAD, (qd + 1) * QUAD)
        q1_ref[d, rs, cs] = (rb[s].astype(F32) + x12[s][:ch, :QUAD]).astype(BF16)
        y0_ref[d, rs, cs] = x3[s][:ch]
        m_ref[d, c, :, cs] = x12[s][ch:, :QUAD] + diag_w[s]
        n0_ref[d, c, :, cs] = x3[s][ch:]


def _rwkv_prep(stage, rv, dg):
    batch, _, _, t, w = stage.shape
    nt = t // TOKEN_TILE
    cpt = TOKEN_TILE // RWKV_CHUNK
    nc = t // RWKV_CHUNK
    dtok = pl.BlockSpec((None, 2, TOKEN_TILE, w), lambda b, j: (b, 0, j, 0))
    mat = pl.BlockSpec((None, 2, cpt, HEAD_DIM, w), lambda b, j: (b, 0, j, 0, 0))
    return pl.pallas_call(
        _rwkv_prep_kernel,
        grid=(batch, nt),
        in_specs=[
            pl.BlockSpec((None, 2, _N_STAGE, TOKEN_TILE, w), lambda b, j: (b, 0, 0, j, 0)),
            pl.BlockSpec((None, TOKEN_TILE, w), lambda b, j: (b, j, 0)),
            pl.BlockSpec((None, 2, cpt, 8, w), lambda b, j: (b, 0, j, 0, 0)),
        ],
        out_specs=[dtok, dtok, mat, mat],
        out_shape=[
            jax.ShapeDtypeStruct((batch, 2, t, w), BF16),
            jax.ShapeDtypeStruct((batch, 2, t, w), F32),
            jax.ShapeDtypeStruct((batch, 2, nc, HEAD_DIM, w), F32),
            jax.ShapeDtypeStruct((batch, 2, nc, HEAD_DIM, w), F32),
        ],
        compiler_params=_params(("parallel", "parallel")),
        name="rwkv_prep",
    )(stage, rv, dg)


CHAIN_CHUNKS = 4


def _rwkv_chain_kernel(q1f_ref, mf_ref, n0f_ref, q1b_ref, mb_ref, n0b_ref, yf_ref, yb_ref, s_ref):
    @pl.when(pl.program_id(0) == 0)
    def _():
        s_ref[...] = jnp.zeros_like(s_ref)

    batch = q1f_ref.shape[0]
    ch = RWKV_CHUNK
    dirs = ((q1f_ref, mf_ref, n0f_ref, yf_ref), (q1b_ref, mb_ref, n0b_ref, yb_ref))

    def split(x):
        hi = x.astype(BF16)
        return hi, (x - hi.astype(F32)).astype(BF16)

    chains = [(d, b, slice(qd * QUAD, (qd + 1) * QUAD))
              for d in range(2) for b in range(batch) for qd in range(BRANCH_WIDTH // QUAD)]
    for sub in range(CHAIN_CHUNKS):
        outs = []
        for d, b, cs in chains:
            q1_ref, m_ref, _, _ = dirs[d]
            k = sub if d == 0 else CHAIN_CHUNKS - 1 - sub
            s_hi, s_lo = split(s_ref[d, b, :, cs])
            m_hi, m_lo = split(m_ref[b, k, :, cs])
            lhs = jnp.concatenate([q1_ref[b, k * ch:(k + 1) * ch, cs], m_hi, m_lo], axis=0)
            o = jnp.dot(lhs, _block_diag(s_hi), preferred_element_type=F32)
            o_lo = jnp.dot(m_hi, _block_diag(s_lo), preferred_element_type=F32)
            outs.append((k, o[:ch], o[ch:2 * ch] + o[2 * ch:] + o_lo))
        for (d, b, cs), (k, y, s_new) in zip(chains, outs):
            _, _, n0_ref, y_ref = dirs[d]
            y_ref[b, k * ch:(k + 1) * ch, cs] = y
            s_ref[d, b, :, cs] = s_new + n0_ref[b, k, :, cs]


def _rwkv_chain(q1, m, n0, n_ctx):
    batch, _, t, w = q1.shape
    rows = RWKV_CHUNK * CHAIN_CHUNKS
    assert n_ctx % rows == 0 and t % rows == 0
    n_blk = t // rows
    n_ctx_blk = n_ctx // rows

    def block(d, s):
        return s if d == 0 else jnp.where(s < n_ctx_blk, n_ctx_blk - 1 - s, n_blk - 1 - (s - n_ctx_blk))

    in_specs = []
    for d in range(2):
        tok = pl.BlockSpec((batch, None, rows, w), lambda s, d=d: (0, d, block(d, s), 0))
        mat = pl.BlockSpec((batch, None, CHAIN_CHUNKS, HEAD_DIM, w), lambda s, d=d: (0, d, block(d, s), 0, 0))
        in_specs += [tok, mat, mat]
    out_specs = [pl.BlockSpec((batch, rows, w), lambda s, d=d: (0, block(d, s), 0)) for d in range(2)]
    return pl.pallas_call(
        _rwkv_chain_kernel,
        grid=(n_blk,),
        in_specs=in_specs,
        out_specs=out_specs,
        out_shape=[jax.ShapeDtypeStruct((batch, t, w), F32)] * 2,
        scratch_shapes=[pltpu.VMEM((2, batch, HEAD_DIM, w), F32)],
        compiler_params=_params(("arbitrary",)),
        name="rwkv_chain",
    )(q1, m, n0, q1, m, n0)


_FF_CHUNK = 256


def _post_kernel(x_ref, mod_ref, gates_ref, ona_ref, yf_ref, yb_ref, y0_ref, g_ref, bv_ref, psgu_ref,
                 gng_ref, gnb_ref, slng_ref, slnb_ref, sw_ref, sb_ref,
                 wb_ref, wo_ref, ln1g_ref, ln1b_ref, wgu_ref, wd_ref, ln2g_ref, ln2b_ref,
                 o_ref, osgu_ref, act_ref, *, alpha):
    w = BRANCH_WIDTH
    d = D_MODEL

    gl = _gelu_tanh(psgu_ref[...].astype(F32))
    u = gl[:, :w]
    v = _layer_norm(gl[:, w:], slng_ref[...], slnb_ref[...], LN_EPS).astype(BF16)
    lo = lax.broadcasted_iota(jnp.int32, (SGU_CHUNK, LANES), 1) < HEAD_DIM
    for ck in range(TOKEN_TILE // SGU_CHUNK):
        rs = slice(ck * SGU_CHUNK, (ck + 1) * SGU_CHUNK)
        for p in range(N_PAIRS):
            cs = slice(p * LANES, (p + 1) * LANES)
            vp = v[rs, cs]
            r0 = jnp.dot(sw_ref[2 * p], vp, preferred_element_type=F32)
            r1 = jnp.dot(sw_ref[2 * p + 1], vp, preferred_element_type=F32)
            osgu_ref[rs, cs] = (u[rs, cs] * (jnp.where(lo, r0, r1) + sb_ref[:, cs])).astype(BF16)

    y = (yf_ref[...] + y0_ref[0]) + (yb_ref[...] + y0_ref[1])
    mu = _head_sum(y) * (1.0 / HEAD_DIM)
    yc = y - mu
    var = _head_sum(yc * yc) * (1.0 / HEAD_DIM)
    y = yc * lax.rsqrt(var + RWKV_GN_EPS) * gng_ref[...] + gnb_ref[...]
    o_rwkv = (y + bv_ref[...]) * g_ref[...]

    acc = _sigmoid(gates_ref[:, 0:d].astype(F32)) * _dot(ona_ref[...], wb_ref[0])
    acc = acc + _sigmoid(gates_ref[:, d:2 * d].astype(F32)) * _dot(o_rwkv, wb_ref[1])
    acc = acc + _sigmoid(gates_ref[:, 2 * d:3 * d].astype(F32)) * jnp.dot(
        osgu_ref[...], wb_ref[2], preferred_element_type=F32)
    z = _dot(acc, wo_ref[...])
    x1 = _layer_norm(alpha * x_ref[...] + mod_ref[2:3, :] * z, ln1g_ref[...], ln1b_ref[...], LN_EPS)

    h = (x1 * (1.0 + mod_ref[4:5, :]) + mod_ref[3:4, :]).astype(BF16)
    for c0 in range(0, D_FF, _FF_CHUNK):
        gate = jnp.dot(h, wgu_ref[:, c0:c0 + _FF_CHUNK], preferred_element_type=F32)
        up = jnp.dot(h, wgu_ref[:, D_FF + c0:D_FF + c0 + _FF_CHUNK], preferred_element_type=F32)
        act_ref[:, c0:c0 + _FF_CHUNK] = (gate * _sigmoid(gate) * up).astype(BF16)
    z = jnp.dot(act_ref[...], wd_ref[...], preferred_element_type=F32)
    o_ref[...] = _layer_norm(alpha * x1 + mod_ref[5:6, :] * z, ln2g_ref[...], ln2b_ref[...], LN_EPS)


def _post(x, mods, gates, o_na, y_f, y_b, y0, g, bv, p_sgu, vecs, sgu_w, sgu_b_full, w_branch, w_out, w_gu, w_down,
          layer, n_ctx_tiles, alpha, skip_tiles):
    batch, t, d = x.shape
    nt = t // TOKEN_TILE - skip_tiles
    w = BRANCH_WIDTH
    row = _mod_row(n_ctx_tiles, batch)
    c2 = lambda b, j: (0, 0)
    tok = lambda n: pl.BlockSpec((None, TOKEN_TILE, n), lambda b, j: (b, j + skip_tiles, 0))
    vw = pl.BlockSpec((1, w), c2)
    vd = pl.BlockSpec((1, d), c2)
    once = dict(pipeline_mode=pl.Buffered(1))
    return pl.pallas_call(
        functools.partial(_post_kernel, alpha=alpha),
        grid=(batch, nt),
        in_specs=[
            tok(d),
            pl.BlockSpec((None, None, 6, d), lambda b, j: (layer, row(b, j + skip_tiles), 0, 0)),
            tok(3 * d), tok(w), tok(w), tok(w),
            pl.BlockSpec((None, 2, TOKEN_TILE, w), lambda b, j: (b, 0, j + skip_tiles, 0)),
            tok(w), tok(w), tok(2 * w),
            vw, vw, vw, vw,
            pl.BlockSpec((None, SGU_GROUPS, SGU_CHUNK, SGU_CHUNK), lambda b, j: (layer, 0, 0, 0)),
            pl.BlockSpec((SGU_CHUNK, w), c2),
            pl.BlockSpec((None, 3, w, d), lambda b, j: (layer, 0, 0, 0), **once),
            pl.BlockSpec((None, d, d), lambda b, j: (layer, 0, 0), **once),
            vd, vd,
            pl.BlockSpec((None, d, 2 * D_FF), lambda b, j: (layer, 0, 0), **once),
            pl.BlockSpec((None, D_FF, d), lambda b, j: (layer, 0, 0), **once),
            vd, vd,
        ],
        out_specs=pl.BlockSpec((None, TOKEN_TILE, d), lambda b, j: (b, j, 0)),
        out_shape=jax.ShapeDtypeStruct((batch, nt * TOKEN_TILE, d), F32),
        scratch_shapes=[pltpu.VMEM((TOKEN_TILE, w), BF16), pltpu.VMEM((TOKEN_TILE, D_FF), BF16)],
        compiler_params=_params(("parallel", "parallel")),
        name="post",
    )(x, mods, gates, o_na, y_f, y_b, y0, g, bv, p_sgu, vecs["gn_g"], vecs["gn_b"], vecs["sgu_ln_g"],
      vecs["sgu_ln_b"], sgu_w, sgu_b_full, w_branch, w_out, vecs["ln1_g"], vecs["ln1_b"], w_gu, w_down,
      vecs["ln2_g"], vecs["ln2_b"])


def kernel(x, c, ctx, c_ctx, w_ada, b_ada, w_in, na_rpb, rwkv_mu_prev, rwkv_mu_next, rwkv_w0, rwkv_w2, rwkv_a0,
           rwkv_a2, rwkv_g2, rwkv_k_k, rwkv_k_a, rwkv_r_k, rwkv_gn_g, rwkv_gn_b, sgu_ln_g, sgu_ln_b, sgu_w, sgu_b,
           w_branch, w_out, ln1_g, ln1_b, ln2_g, ln2_b, ffn_w_gu, ffn_w_down):
    batch, seq, d = x.shape
    n_ctx = ctx.shape[1]
    depth = w_ada.shape[0]
    assert d == D_MODEL and batch <= 7
    assert n_ctx % TOKEN_TILE == 0 and seq % TOKEN_TILE == 0
    assert seq // GRID_W >= NA_WIN_H
    n_ctx_tiles = n_ctx // TOKEN_TILE
    alpha = float((2 * depth) ** 0.25)

    cond = jnp.zeros((8, d), F32).at[:batch].set(c).at[batch].set(c_ctx)
    mods = jnp.transpose(_ada_all(cond, w_ada, b_ada), (0, 2, 1, 3))

    xs = jnp.concatenate([ctx, x], axis=1)
    w_in_b = w_in.astype(BF16)
    w_branch_b = w_branch.astype(BF16)
    w_out_b = w_out.astype(BF16)
    w_gu_b = ffn_w_gu.astype(BF16)
    w_down_b = ffn_w_down.astype(BF16)
    sgu_w_b = sgu_w.astype(BF16)
    row = lambda a: a.reshape(1, -1)

    for l in range(depth):
        last = l == depth - 1
        prm = dict(
            mu_prev=row(rwkv_mu_prev[l]), mu_next=row(rwkv_mu_next[l]),
            w0=rwkv_w0[l], w2=rwkv_w2[l].reshape(2 * DECAY_LORA, BRANCH_WIDTH).astype(BF16),
            a0=rwkv_a0[l], a2=rwkv_a2[l].reshape(2 * DECAY_LORA, BRANCH_WIDTH).astype(BF16),
            g2=rwkv_g2[l].astype(BF16), k_k=row(rwkv_k_k[l]), k_a=row(rwkv_k_a[l]), r_k=row(rwkv_r_k[l]))
        gates, q, k, v, p_sgu, stage, rv, dg, g, bv = _inproj(xs, mods, w_in_b, prm, l, n_ctx_tiles)
        o_na = _na_attention(q, k, v, _na_bias_table(na_rpb[l]), n_ctx)
        q1, y0, m, n0 = _rwkv_prep(stage, rv, dg)
        y_f, y_b = _rwkv_chain(q1, m, n0, n_ctx)
        b_full = jnp.repeat(jnp.transpose(sgu_b[l]), BRANCH_WIDTH // SGU_GROUPS, axis=1)
        vecs = dict(gn_g=row(rwkv_gn_g[l]), gn_b=row(rwkv_gn_b[l]), sgu_ln_g=row(sgu_ln_g[l]),
                    sgu_ln_b=row(sgu_ln_b[l]), ln1_g=row(ln1_g[l]), ln1_b=row(ln1_b[l]), ln2_g=row(ln2_g[l]),
                    ln2_b=row(ln2_b[l]))
        xs = _post(xs, mods, gates, o_na, y_f, y_b, y0, g, bv, p_sgu, vecs, sgu_w_b, b_full, w_branch_b, w_out_b,
                   w_gu_b, w_down_b, l, n_ctx_tiles, alpha, n_ctx_tiles if last else 0)
    return xs
```

```python
import functools

import jax
import jax.numpy as jnp
import numpy as np
from jax import lax
from jax.experimental import pallas as pl
from jax.experimental.pallas import tpu as pltpu

F32 = jnp.float32
BF16 = jnp.bfloat16

D_MODEL = 1024
GRID_W = 64
NA_HEADS = 8
HEAD_DIM = 64
NA_WIN_H = 8
NA_WIN_W = 16
BRANCH_WIDTH = 512
DECAY_LORA = 64
GATE_LORA = 128
RWKV_IN = 3 * BRANCH_WIDTH + 4 * DECAY_LORA + GATE_LORA
RWKV_GN_EPS = 64e-5
SGU_CHUNK = 128
SGU_GROUPS = 8
D_FF = 2816
LN_EPS = 1e-5
MASK_VALUE = -1e30

LANES = 128
TOKEN_TILE = 256
RWKV_CHUNK = 64
N_PAIRS = NA_HEADS // 2
QUAD = 4 * HEAD_DIM
HIGHEST = lax.Precision.HIGHEST
VMEM_LIMIT = 56 * 1024 * 1024

_NT = (((1,), (1,)), ((), ()))


def _params(sem, vmem=VMEM_LIMIT):
    return pltpu.CompilerParams(dimension_semantics=sem, vmem_limit_bytes=vmem)


def _dot(a, b):
    return jnp.dot(a.astype(BF16), b.astype(BF16), preferred_element_type=F32)


def _layer_norm(x, g, b, eps):
    mu = jnp.mean(x, axis=-1, keepdims=True)
    xc = x - mu
    var = jnp.mean(xc * xc, axis=-1, keepdims=True)
    return xc * lax.rsqrt(var + eps) * g + b


def _sigmoid(x):
    return 0.5 * jnp.tanh(0.5 * x) + 0.5


def _head_sum(x):
    rows = x.shape[0]
    lo = lax.broadcasted_iota(jnp.int32, (rows, LANES), 1) < HEAD_DIM
    out = []
    for p in range(x.shape[1] // LANES):
        xp = x[:, p * LANES:(p + 1) * LANES]
        s_lo = jnp.sum(jnp.where(lo, xp, 0.0), axis=-1, keepdims=True)
        s_hi = jnp.sum(jnp.where(lo, 0.0, xp), axis=-1, keepdims=True)
        out.append(jnp.where(lo, s_lo, s_hi))
    return jnp.concatenate(out, axis=1)


def _block_diag(xb):
    rows = xb.shape[0]
    lane = lax.broadcasted_iota(jnp.int32, (rows, LANES), 1)
    lo_b = jnp.where(lane < HEAD_DIM, 1.0, 0.0).astype(BF16)
    hi_b = jnp.where(lane >= HEAD_DIM, 1.0, 0.0).astype(BF16)
    zero_b = jnp.zeros((rows, LANES), BF16)
    t0, t1 = xb[:, :LANES], xb[:, LANES:]
    return jnp.concatenate([
        jnp.concatenate([t0 * lo_b, zero_b], axis=1), jnp.concatenate([t0 * hi_b, zero_b], axis=1),
        jnp.concatenate([zero_b, t1 * lo_b], axis=1), jnp.concatenate([zero_b, t1 * hi_b], axis=1)], axis=0)


def _ada_kernel(c_ref, w_ref, b_ref, o_ref):
    c = c_ref[...]
    o_ref[...] = _dot(c * _sigmoid(c), w_ref[...]) + b_ref[...]


def _ada_all(cond, w_ada, b_ada):
    depth, d, n = w_ada.shape
    n_mod = n // d
    return pl.pallas_call(
        _ada_kernel,
        grid=(depth, n_mod),
        in_specs=[
            pl.BlockSpec((8, d), lambda l, i: (0, 0)),
            pl.BlockSpec((None, d, d), lambda l, i: (l, 0, i)),
            pl.BlockSpec((None, 1, d), lambda l, i: (l, 0, i)),
        ],
        out_specs=pl.BlockSpec((None, None, 8, d), lambda l, i: (l, i, 0, 0)),
        out_shape=jax.ShapeDtypeStruct((depth, n_mod, 8, d), F32),
        compiler_params=_params(("parallel", "parallel")),
        name="adaln",
    )(cond, w_ada, b_ada.reshape(depth, 1, n))


_IN_SPLITS = (0, 3 * D_MODEL, 3 * D_MODEL + 512, 3 * D_MODEL + 1024, 3 * D_MODEL + 1536,
              3 * D_MODEL + 1536 + RWKV_IN, 3 * D_MODEL + 1536 + RWKV_IN + 2 * BRANCH_WIDTH)
_IN_COL_CHUNK = 512
_HALO = 8
_N_STAGE = 4


def _inproj_kernel(x_ref, xp_ref, xn_ref, mod_ref, w_ref, mup_ref, mun_ref, w0_ref, w2_ref, a0_ref, a2_ref,
                   g2_ref, kk_ref, ka_ref, rk_ref,
                   gates_ref, q_ref, k_ref, v_ref, ps_ref, stage_ref, rv_ref, dg_ref, g_ref, bv_ref, p_scr,
                   *, n_ctx_tiles, n_tiles):
    j = pl.program_id(1)
    tm = TOKEN_TILE
    w = BRANCH_WIDTH
    ch = RWKV_CHUNK
    shift = mod_ref[0:1, :]
    scale1 = 1.0 + mod_ref[1:2, :]
    h = (x_ref[...] * scale1 + shift).astype(BF16)

    h_ext = jnp.concatenate([(xp_ref[...] * scale1 + shift).astype(BF16), h,
                             (xn_ref[...] * scale1 + shift).astype(BF16)], axis=0)
    lo_r, hi_r = _IN_SPLITS[4], _IN_SPLITS[5]
    for c0 in range(lo_r, hi_r, _IN_COL_CHUNK):
        c1 = min(c0 + _IN_COL_CHUNK, hi_r)
        p_scr[:, c0 - lo_r:c1 - lo_r] = jnp.dot(h_ext, w_ref[:, c0:c1], preferred_element_type=F32)

    def project():
        outs = ((0, gates_ref), (1, q_ref), (2, k_ref), (3, v_ref), (5, ps_ref))
        for idx, o_ref in outs:
            lo, hi = _IN_SPLITS[idx], _IN_SPLITS[idx + 1]
            scale = HEAD_DIM ** -0.5 if idx == 1 else None
            for c0 in range(lo, hi, _IN_COL_CHUNK):
                c1 = min(c0 + _IN_COL_CHUNK, hi)
                r = jnp.dot(h, w_ref[:, c0:c1], preferred_element_type=F32)
                if scale is not None:
                    r = r * scale
                o_ref[:, c0 - lo:c1 - lo] = r.astype(o_ref.dtype)
                yield

    lo_dir = lax.broadcasted_iota(jnp.int32, (tm, LANES), 1) < DECAY_LORA
    ti = lax.broadcasted_iota(jnp.int32, (tm, tm), 0)
    si = lax.broadcasted_iota(jnp.int32, (tm, tm), 1)
    same_chunk = (ti // ch) == (si // ch)

    def stage_tile():
        p = p_scr[_HALO:_HALO + tm, :]
        first = (j == 0) | (j == n_ctx_tiles)
        last = (j == n_ctx_tiles - 1) | (j == n_tiles - 1)
        prev_row = jnp.where(first, 0.0, p_scr[_HALO - 1:_HALO, :])
        next_row = jnp.where(last, 0.0, p_scr[_HALO + tm:_HALO + tm + 1, :])
        row8 = lax.broadcasted_iota(jnp.int32, (8, 1), 0)
        down = pltpu.roll(p, 1, 0)
        up = pltpu.roll(p, tm - 1, 0)
        p_prev = jnp.concatenate([jnp.where(row8 == 0, prev_row, down[:8]), down[8:]], axis=0)
        p_next = jnp.concatenate([up[:tm - 8], jnp.where(row8 == 7, next_row, up[tm - 8:])], axis=0)
        ps = (1.0 - mup_ref[...] - mun_ref[...]) * p + mup_ref[...] * p_prev + mun_ref[...] * p_next
        yield
        r = ps[:, 0:w]
        k = ps[:, w:2 * w]
        v = ps[:, 2 * w:3 * w]
        wc = jnp.tanh(ps[:, 3 * w:3 * w + LANES])
        ac = ps[:, 3 * w + LANES:3 * w + 2 * LANES]
        gc = ps[:, 3 * w + 2 * LANES:3 * w + 3 * LANES]
        g_ref[...] = _dot(_sigmoid(gc), g2_ref[...])
        rv_ref[...] = v.astype(BF16)
        yield
        kk = k * kk_ref[...]
        kk = kk * lax.rsqrt(_head_sum(kk * kk) + 1e-12)
        yield
        kd_sum = jnp.zeros((tm, w), F32)
        for d in range(2):
            md = lo_dir if d == 0 else ~lo_dir
            z = w0_ref[d:d + 1, :] + _dot(jnp.where(md, wc, 0.0), w2_ref[...])
            lw = -float(np.exp(-0.5)) * _sigmoid(z)
            a = _sigmoid(a0_ref[d:d + 1, :] + _dot(jnp.where(md, ac, 0.0), a2_ref[...]))
            yield
            kd = k * (1.0 + (a - 1.0) * ka_ref[...])
            kd_sum = kd_sum + kd
            beta = kk * a
            before = (si <= ti) if d == 0 else (si >= ti)
            tri = jnp.where(same_chunk & before, 1.0, 0.0).astype(BF16)
            lw_hi = lw.astype(BF16)
            rem = lw - lw_hi.astype(F32)
            lw_mid = rem.astype(BF16)
            lw_lo = (rem - lw_mid.astype(F32)).astype(BF16)
            parts = jnp.dot(tri, jnp.concatenate([lw_hi, lw_mid, lw_lo], axis=1), preferred_element_type=F32)
            cum = (parts[:, 2 * w:] + parts[:, w:2 * w]) + parts[:, :w]
            yield
            e_neg = jnp.exp(-cum)
            stage_ref[d, 0] = (-kk * jnp.exp(cum - lw)).astype(BF16)
            stage_ref[d, 1] = (r * jnp.exp(cum)).astype(BF16)
            yield
            stage_ref[d, 2] = (beta * e_neg).astype(BF16)
            stage_ref[d, 3] = (kd * e_neg).astype(BF16)
            last_step = ch - 1 if d == 0 else 0
            for c in range(tm // ch):
                tot = cum[c * ch + last_step:c * ch + last_step + 1]
                dg_ref[d, c] = jnp.broadcast_to(jnp.exp(tot), (8, w))
            yield
        bv_ref[...] = _head_sum(r * kd_sum * rk_ref[...]) * v
        yield

    gens = [project(), stage_tile()]
    while gens:
        for gen in list(gens):
            if next(gen, StopIteration) is StopIteration:
                gens.remove(gen)


def _mod_row(n_ctx_tiles, batch):
    return lambda b, j: jnp.where(j < n_ctx_tiles, batch, b)


def _inproj(x, mods, w_in, prm, layer, n_ctx_tiles):
    batch, t, d = x.shape
    nt = t // TOKEN_TILE
    cpt = TOKEN_TILE // RWKV_CHUNK
    nc = t // RWKV_CHUNK
    d_in = w_in.shape[-1]
    w = BRANCH_WIDTH
    halo = TOKEN_TILE // _HALO
    row = _mod_row(n_ctx_tiles, batch)
    c2 = lambda b, j: (0, 0)
    vec = lambda n: pl.BlockSpec((1, n), c2)
    tok = lambda n: pl.BlockSpec((None, TOKEN_TILE, n), lambda b, j: (b, j, 0))
    widths = [_IN_SPLITS[i + 1] - _IN_SPLITS[i] for i in (0, 1, 2, 3, 5)]
    return pl.pallas_call(
        functools.partial(_inproj_kernel, n_ctx_tiles=n_ctx_tiles, n_tiles=nt),
        grid=(batch, nt),
        in_specs=[
            tok(d),
            pl.BlockSpec((None, _HALO, d), lambda b, j: (b, jnp.maximum(j * halo - 1, 0), 0)),
            pl.BlockSpec((None, _HALO, d), lambda b, j: (b, jnp.minimum((j + 1) * halo, t // _HALO - 1), 0)),
            pl.BlockSpec((None, None, 6, d), lambda b, j: (layer, row(b, j), 0, 0)),
            pl.BlockSpec((None, d, d_in), lambda b, j: (layer, 0, 0), pipeline_mode=pl.Buffered(1)),
            vec(RWKV_IN), vec(RWKV_IN),
            pl.BlockSpec((2, w), c2), pl.BlockSpec((2 * DECAY_LORA, w), c2),
            pl.BlockSpec((2, w), c2), pl.BlockSpec((2 * DECAY_LORA, w), c2),
            pl.BlockSpec((GATE_LORA, w), c2),
            vec(w), vec(w), vec(w),
        ],
        out_specs=[tok(n) for n in widths] + [
            pl.BlockSpec((None, 2, _N_STAGE, TOKEN_TILE, w), lambda b, j: (b, 0, 0, j, 0)),
            tok(w),
            pl.BlockSpec((None, 2, cpt, 8, w), lambda b, j: (b, 0, j, 0, 0)),
            tok(w), tok(w)],
        out_shape=[jax.ShapeDtypeStruct((batch, t, n), BF16) for n in widths] + [
            jax.ShapeDtypeStruct((batch, 2, _N_STAGE, t, w), BF16),
            jax.ShapeDtypeStruct((batch, t, w), BF16),
            jax.ShapeDtypeStruct((batch, 2, nc, 8, w), F32),
            jax.ShapeDtypeStruct((batch, t, w), F32),
            jax.ShapeDtypeStruct((batch, t, w), F32)],
        scratch_shapes=[pltpu.VMEM((TOKEN_TILE + 2 * _HALO, RWKV_IN), F32)],
        compiler_params=_params(("parallel", "parallel")),
        name="inproj",
    )(x, x, x, mods, w_in, prm["mu_prev"], prm["mu_next"], prm["w0"], prm["w2"], prm["a0"], prm["a2"],
      prm["g2"], prm["k_k"], prm["k_a"], prm["r_k"])


def _na_bias_table(rpb):
    cols = np.arange(GRID_W)
    col0 = np.clip(cols - NA_WIN_W // 2, 0, GRID_W - NA_WIN_W)
    kc = np.arange(GRID_W)
    in_win = (kc[None, :] >= col0[:, None]) & (kc[None, :] < col0[:, None] + NA_WIN_W)
    dcol = kc[None, :] - cols[:, None] + NA_WIN_W - 1
    n_dcol = 2 * NA_WIN_W - 1
    onehot = ((dcol[None] == np.arange(n_dcol)[:, None, None]) & in_win[None]).astype(np.float32)
    toe = jnp.einsum("hrd,dck->hrck", rpb, jnp.asarray(onehot), precision=HIGHEST)
    toe = toe + jnp.asarray(np.where(in_win, 0.0, MASK_VALUE).astype(np.float32))
    two = jnp.concatenate([toe[:, :-1], toe[:, 1:]], axis=-1)
    two = two.reshape(N_PAIRS, 2, 2 * NA_WIN_H - 2, GRID_W, 2 * GRID_W)
    return jnp.transpose(two, (0, 2, 1, 3, 4)).reshape(N_PAIRS, 2 * NA_WIN_H - 2, 2 * GRID_W, 2 * GRID_W)


NA_ROWS_PER_STEP = 4


def _na_kernel(q_ref, k_ref, v_ref, bias_ref, o_ref, *, n_ctx, rows):
    j = pl.program_id(1)
    ctx_steps = n_ctx // (GRID_W * NA_ROWS_PER_STEP)
    lo2 = lax.broadcasted_iota(jnp.int32, (2 * GRID_W, LANES), 1) < HEAD_DIM
    top = lax.broadcasted_iota(jnp.int32, (2 * GRID_W, LANES), 0) < GRID_W
    own_head = lo2 == top
    lo = lax.broadcasted_iota(jnp.int32, (GRID_W, LANES), 1) < HEAD_DIM
    n_win = NA_WIN_H * GRID_W
    pairs = [slice(p * LANES, (p + 1) * LANES) for p in range(N_PAIRS)]

    def attend(local):
        n_rows = NA_ROWS_PER_STEP
        q2 = []
        for cs in pairs:
            parts = []
            for i in range(n_rows):
                qp = q_ref[i * GRID_W:(i + 1) * GRID_W, cs].astype(F32)
                parts.append(jnp.where(own_head, jnp.concatenate([qp, qp], axis=0), 0.0).astype(BF16))
            q2.append(parts)
        s_c_all = [lax.dot_general(jnp.concatenate(parts, axis=0), k_ref[0:n_ctx, cs], _NT,
                                   preferred_element_type=F32) for parts, cs in zip(q2, pairs)]
        p_c_rows, den_rows, acc_l_rows = [], [], []
        for i in range(n_rows):
            rs = slice(i * 2 * GRID_W, (i + 1) * 2 * GRID_W)
            s_c = [s[rs] for s in s_c_all]
            mx = [jnp.max(s, axis=-1, keepdims=True) for s in s_c]
            if local:
                r = (j - ctx_steps) * n_rows + i
                row0 = jnp.clip(r - NA_WIN_H // 2, 0, rows - NA_WIN_H)
                start = pl.multiple_of(n_ctx + row0 * GRID_W, GRID_W)
                off0 = NA_WIN_H - 1 - (r - row0)
                s_l = [lax.dot_general(parts[i], k_ref[pl.ds(start, n_win), cs], _NT, preferred_element_type=F32)
                       + jnp.concatenate([bias_ref[p, off0 + 2 * m] for m in range(NA_WIN_H // 2)], axis=1)
                       for p, (parts, cs) in enumerate(zip(q2, pairs))]
                mx = [jnp.maximum(m, jnp.max(s, axis=-1, keepdims=True)) for m, s in zip(mx, s_l)]
                p_l = [jnp.exp(s - m) for s, m in zip(s_l, mx)]
            p_c = [jnp.exp(s - m) for s, m in zip(s_c, mx)]
            den = [jnp.sum(x, axis=-1, keepdims=True) for x in p_c]
            if local:
                den = [d_ + jnp.sum(x, axis=-1, keepdims=True) for d_, x in zip(den, p_l)]
                acc_l_rows.append([jnp.dot(x.astype(BF16), v_ref[pl.ds(start, n_win), cs],
                                           preferred_element_type=F32) for x, cs in zip(p_l, pairs)])
            p_c_rows.append([x.astype(BF16) for x in p_c])
            den_rows.append(den)
        acc_c_all = [jnp.dot(jnp.concatenate([p_c_rows[i][p] for i in range(n_rows)], axis=0), v_ref[0:n_ctx, cs],
                             preferred_element_type=F32) for p, cs in enumerate(pairs)]
        for i in range(n_rows):
            rs = slice(i * 2 * GRID_W, (i + 1) * 2 * GRID_W)
            for p, cs in enumerate(pairs):
                a = acc_c_all[p][rs]
                if local:
                    a = a + acc_l_rows[i][p]
                o = a / den_rows[i][p]
                o_ref[i * GRID_W:(i + 1) * GRID_W, cs] = jnp.where(lo, o[:GRID_W], o[GRID_W:])

    @pl.when(j < ctx_steps)
    def _():
        attend(False)

    @pl.when(j >= ctx_steps)
    def _():
        attend(True)


def _na_attention(q, k, v, bias, n_ctx):
    batch, t, w = q.shape
    rows = (t - n_ctx) // GRID_W
    tq = GRID_W * NA_ROWS_PER_STEP
    assert n_ctx % tq == 0 and rows % NA_ROWS_PER_STEP == 0
    return pl.pallas_call(
        functools.partial(_na_kernel, n_ctx=n_ctx, rows=rows),
        grid=(batch, t // tq),
        in_specs=[
            pl.BlockSpec((None, tq, w), lambda b, j: (b, j, 0)),
            pl.BlockSpec((None, t, w), lambda b, j: (b, 0, 0)),
            pl.BlockSpec((None, t, w), lambda b, j: (b, 0, 0)),
            pl.BlockSpec(bias.shape, lambda b, j: (0, 0, 0, 0)),
        ],
        out_specs=pl.BlockSpec((None, tq, w), lambda b, j: (b, j, 0)),
        out_shape=jax.ShapeDtypeStruct((batch, t, w), F32),
        compiler_params=_params(("parallel", "arbitrary")),
        name="na_attention",
    )(q, k, v, bias)


def _gelu_tanh(x):
    return x * (0.5 * (1.0 + jnp.tanh(0.7978845608028654 * (x + 0.044715 * (x * x * x)))))


def _rwkv_prep_kernel(stage_ref, rv_ref, dg_ref, q1_ref, y0_ref, m_ref, n0_ref):
    tm = TOKEN_TILE
    w = BRANCH_WIDTH
    ch = RWKV_CHUNK
    ci = lax.broadcasted_iota(jnp.int32, (ch, QUAD), 0)
    cj = lax.broadcasted_iota(jnp.int32, (ch, QUAD), 1) % HEAD_DIM
    eye_w = ci == cj
    eye_b = jnp.where(eye_w, 1.0, 0.0).astype(BF16)
    bi = lax.broadcasted_iota(jnp.int32, (QUAD, QUAD), 0) // HEAD_DIM
    bj = lax.broadcasted_iota(jnp.int32, (QUAD, QUAD), 1) // HEAD_DIM
    maskb = jnp.where(bi == bj, 1.0, 0.0).astype(BF16)

    strict = ((cj < ci), (cj > ci))
    incl = ((cj <= ci), (cj >= ci))

    def bd(x):
        return _block_diag(x.astype(BF16))

    def wide_dot(lhs, rhs):
        return jnp.dot(lhs.astype(BF16), rhs, preferred_element_type=F32)

    chains = [(d, c, qd) for c in range(tm // ch) for d in range(2) for qd in range(w // QUAD)]
    n_ch = len(chains)
    dirs = [d for d, _, _ in chains]

    def load(s):
        return [stage_ref[d, s, c * ch:(c + 1) * ch, qd * QUAD:(qd + 1) * QUAD] for d, c, qd in chains]

    ab, rb, bt, kt = load(0), load(1), load(2), load(3)
    lhs = [jnp.concatenate([a_, r_, eye_b], axis=0) for a_, r_ in zip(ab, rb)]
    g1 = [lax.dot_general(l_, _block_diag(x), _NT, preferred_element_type=F32) for l_, x in zip(lhs, bt)]
    g2 = [lax.dot_general(l_, _block_diag(x), _NT, preferred_element_type=F32) for l_, x in zip(lhs, kt)]
    diag_w = [jnp.where(eye_w, dg_ref[d, c, 0:1, qd * QUAD:(qd + 1) * QUAD], 0.0) for d, c, qd in chains]
    scale = wide_dot(jnp.concatenate(diag_w, axis=0), maskb)
    scale = [scale[s * ch:(s + 1) * ch] for s in range(n_ch)]
    apow = [jnp.where(strict[d], g[:ch], 0.0) for d, g in zip(dirs, g1)]
    h_rb = [jnp.where(incl[d], g[ch:2 * ch], 0.0) for d, g in zip(dirs, g1)]
    j_bh = [s_ * g[2 * ch:] for s_, g in zip(scale, g1)]
    a_ak = [jnp.where(strict[d], g[:ch], 0.0) for d, g in zip(dirs, g2)]
    tail = [jnp.concatenate([jnp.where(incl[d], g[ch:2 * ch], 0.0), s_ * g[2 * ch:]], axis=0)
            for d, s_, g in zip(dirs, scale, g2)]
    for step in range(6):
        grow = step < 5
        stacked = [jnp.concatenate(([p_] if grow else []) + [h_, j_], axis=0)
                   for p_, h_, j_ in zip(apow, h_rb, j_bh)]
        out = [wide_dot(l_, bd(p_)) for l_, p_ in zip(stacked, apow)]
        off = ch if grow else 0
        if grow:
            apow = [o[:ch] for o in out]
        h_rb = [x + o[off:off + ch] for x, o in zip(h_rb, out)]
        j_bh = [x + o[off + ch:] for x, o in zip(j_bh, out)]
    hj = [jnp.concatenate([h_, j_], axis=0).astype(BF16) for h_, j_ in zip(h_rb, j_bh)]
    x12 = [jnp.dot(l_, jnp.concatenate([_block_diag(a_), bd(k_)], axis=1), preferred_element_type=F32)
           for l_, a_, k_ in zip(hj, ab, a_ak)]
    vv = [rv_ref[c * ch:(c + 1) * ch, qd * QUAD:(qd + 1) * QUAD] for _, c, qd in chains]
    x3 = [wide_dot(x[:, QUAD:] + t_, _block_diag(v_)) for x, t_, v_ in zip(x12, tail, vv)]
    for s, (d, c, qd) in enumerate(chains):
        rs = slice(c * ch, (c + 1) * ch)
        cs = slice(qd * QUAD, (qd + 1) * QUAD)
        q1_ref[d, rs, cs] = (rb[s].astype(F32) + x12[s][:ch, :QUAD]).astype(BF16)
        y0_ref[d, rs, cs] = x3[s][:ch]
        m_ref[d, c, :, cs] = x12[s][ch:, :QUAD] + diag_w[s]
        n0_ref[d, c, :, cs] = x3[s][ch:]


def _rwkv_prep(stage, rv, dg):
    batch, _, _, t, w = stage.shape
    nt = t // TOKEN_TILE
    cpt = TOKEN_TILE // RWKV_CHUNK
    nc = t // RWKV_CHUNK
    dtok = pl.BlockSpec((None, 2, TOKEN_TILE, w), lambda b, j: (b, 0, j, 0))
    mat = pl.BlockSpec((None, 2, cpt, HEAD_DIM, w), lambda b, j: (b, 0, j, 0, 0))
    return pl.pallas_call(
        _rwkv_prep_kernel,
        grid=(batch, nt),
        in_specs=[
            pl.BlockSpec((None, 2, _N_STAGE, TOKEN_TILE, w), lambda b, j: (b, 0, 0, j, 0)),
            pl.BlockSpec((None, TOKEN_TILE, w), lambda b, j: (b, j, 0)),
            pl.BlockSpec((None, 2, cpt, 8, w), lambda b, j: (b, 0, j, 0, 0)),
        ],
        out_specs=[dtok, dtok, mat, mat],
        out_shape=[
            jax.ShapeDtypeStruct((batch, 2, t, w), BF16),
            jax.ShapeDtypeStruct((batch, 2, t, w), F32),
            jax.ShapeDtypeStruct((batch, 2, nc, HEAD_DIM, w), F32),
            jax.ShapeDtypeStruct((batch, 2, nc, HEAD_DIM, w), F32),
        ],
        compiler_params=_params(("parallel", "parallel")),
        name="rwkv_prep",
    )(stage, rv, dg)


CHAIN_CHUNKS = 4


def _rwkv_chain_kernel(q1f_ref, mf_ref, n0f_ref, q1b_ref, mb_ref, n0b_ref, yf_ref, yb_ref, s_ref):
    @pl.when(pl.program_id(0) == 0)
    def _():
        s_ref[...] = jnp.zeros_like(s_ref)

    batch = q1f_ref.shape[0]
    ch = RWKV_CHUNK
    dirs = ((q1f_ref, mf_ref, n0f_ref, yf_ref), (q1b_ref, mb_ref, n0b_ref, yb_ref))

    def split(x):
        hi = x.astype(BF16)
        return hi, (x - hi.astype(F32)).astype(BF16)

    chains = [(d, b, slice(qd * QUAD, (qd + 1) * QUAD))
              for d in range(2) for b in range(batch) for qd in range(BRANCH_WIDTH // QUAD)]
    for sub in range(CHAIN_CHUNKS):
        outs = []
        for d, b, cs in chains:
            q1_ref, m_ref, _, _ = dirs[d]
            k = sub if d == 0 else CHAIN_CHUNKS - 1 - sub
            s_hi, s_lo = split(s_ref[d, b, :, cs])
            m_hi, m_lo = split(m_ref[b, k, :, cs])
            lhs = jnp.concatenate([q1_ref[b, k * ch:(k + 1) * ch, cs], m_hi, m_lo], axis=0)
            o = jnp.dot(lhs, _block_diag(s_hi), preferred_element_type=F32)
            o_lo = jnp.dot(m_hi, _block_diag(s_lo), preferred_element_type=F32)
            outs.append((k, o[:ch], o[ch:2 * ch] + o[2 * ch:] + o_lo))
        for (d, b, cs), (k, y, s_new) in zip(chains, outs):
            _, _, n0_ref, y_ref = dirs[d]
            y_ref[b, k * ch:(k + 1) * ch, cs] = y
            s_ref[d, b, :, cs] = s_new + n0_ref[b, k, :, cs]


def _rwkv_chain(q1, m, n0, n_ctx):
    batch, _, t, w = q1.shape
    rows = RWKV_CHUNK * CHAIN_CHUNKS
    assert n_ctx % rows == 0 and t % rows == 0
    n_blk = t // rows
    n_ctx_blk = n_ctx // rows

    def block(d, s):
        return s if d == 0 else jnp.where(s < n_ctx_blk, n_ctx_blk - 1 - s, n_blk - 1 - (s - n_ctx_blk))

    in_specs = []
    for d in range(2):
        tok = pl.BlockSpec((batch, None, rows, w), lambda s, d=d: (0, d, block(d, s), 0))
        mat = pl.BlockSpec((batch, None, CHAIN_CHUNKS, HEAD_DIM, w), lambda s, d=d: (0, d, block(d, s), 0, 0))
        in_specs += [tok, mat, mat]
    out_specs = [pl.BlockSpec((batch, rows, w), lambda s, d=d: (0, block(d, s), 0)) for d in range(2)]
    return pl.pallas_call(
        _rwkv_chain_kernel,
        grid=(n_blk,),
        in_specs=in_specs,
        out_specs=out_specs,
        out_shape=[jax.ShapeDtypeStruct((batch, t, w), F32)] * 2,
        scratch_shapes=[pltpu.VMEM((2, batch, HEAD_DIM, w), F32)],
        compiler_params=_params(("arbitrary",)),
        name="rwkv_chain",
    )(q1, m, n0, q1, m, n0)


_FF_CHUNK = 256


def _post_kernel(x_ref, mod_ref, gates_ref, ona_ref, yf_ref, yb_ref, y0_ref, g_ref, bv_ref, psgu_ref,
                 gng_ref, gnb_ref, slng_ref, slnb_ref, sw_ref, sb_ref,
                 wb_ref, wo_ref, ln1g_ref, ln1b_ref, wgu_ref, wd_ref, ln2g_ref, ln2b_ref,
                 o_ref, osgu_ref, act_ref, *, alpha):
    w = BRANCH_WIDTH
    d = D_MODEL

    gl = _gelu_tanh(psgu_ref[...].astype(F32))
    u = gl[:, :w]
    v = _layer_norm(gl[:, w:], slng_ref[...], slnb_ref[...], LN_EPS).astype(BF16)
    lo = lax.broadcasted_iota(jnp.int32, (SGU_CHUNK, LANES), 1) < HEAD_DIM
    for ck in range(TOKEN_TILE // SGU_CHUNK):
        rs = slice(ck * SGU_CHUNK, (ck + 1) * SGU_CHUNK)
        for p in range(N_PAIRS):
            cs = slice(p * LANES, (p + 1) * LANES)
            vp = v[rs, cs]
            r0 = jnp.dot(sw_ref[2 * p], vp, preferred_element_type=F32)
            r1 = jnp.dot(sw_ref[2 * p + 1], vp, preferred_element_type=F32)
            osgu_ref[rs, cs] = (u[rs, cs] * (jnp.where(lo, r0, r1) + sb_ref[:, cs])).astype(BF16)

    y = (yf_ref[...] + y0_ref[0]) + (yb_ref[...] + y0_ref[1])
    mu = _head_sum(y) * (1.0 / HEAD_DIM)
    yc = y - mu
    var = _head_sum(yc * yc) * (1.0 / HEAD_DIM)
    y = yc * lax.rsqrt(var + RWKV_GN_EPS) * gng_ref[...] + gnb_ref[...]
    o_rwkv = (y + bv_ref[...]) * g_ref[...]

    acc = _sigmoid(gates_ref[:, 0:d].astype(F32)) * _dot(ona_ref[...], wb_ref[0])
    acc = acc + _sigmoid(gates_ref[:, d:2 * d].astype(F32)) * _dot(o_rwkv, wb_ref[1])
    acc = acc + _sigmoid(gates_ref[:, 2 * d:3 * d].astype(F32)) * jnp.dot(
        osgu_ref[...], wb_ref[2], preferred_element_type=F32)
    z = _dot(acc, wo_ref[...])
    x1 = _layer_norm(alpha * x_ref[...] + mod_ref[2:3, :] * z, ln1g_ref[...], ln1b_ref[...], LN_EPS)

    h = (x1 * (1.0 + mod_ref[4:5, :]) + mod_ref[3:4, :]).astype(BF16)
    for c0 in range(0, D_FF, _FF_CHUNK):
        gate = jnp.dot(h, wgu_ref[:, c0:c0 + _FF_CHUNK], preferred_element_type=F32)
        up = jnp.dot(h, wgu_ref[:, D_FF + c0:D_FF + c0 + _FF_CHUNK], preferred_element_type=F32)
        act_ref[:, c0:c0 + _FF_CHUNK] = (gate * _sigmoid(gate) * up).astype(BF16)
    z = jnp.dot(act_ref[...], wd_ref[...], preferred_element_type=F32)
    o_ref[...] = _layer_norm(alpha * x1 + mod_ref[5:6, :] * z, ln2g_ref[...], ln2b_ref[...], LN_EPS)


def _post(x, mods, gates, o_na, y_f, y_b, y0, g, bv, p_sgu, vecs, sgu_w, sgu_b_full, w_branch, w_out, w_gu, w_down,
          layer, n_ctx_tiles, alpha, skip_tiles):
    batch, t, d = x.shape
    nt = t // TOKEN_TILE - skip_tiles
    w = BRANCH_WIDTH
    row = _mod_row(n_ctx_tiles, batch)
    c2 = lambda b, j: (0, 0)
    tok = lambda n: pl.BlockSpec((None, TOKEN_TILE, n), lambda b, j: (b, j + skip_tiles, 0))
    vw = pl.BlockSpec((1, w), c2)
    vd = pl.BlockSpec((1, d), c2)
    once = dict(pipeline_mode=pl.Buffered(1))
    return pl.pallas_call(
        functools.partial(_post_kernel, alpha=alpha),
        grid=(batch, nt),
        in_specs=[
            tok(d),
            pl.BlockSpec((None, None, 6, d), lambda b, j: (layer, row(b, j + skip_tiles), 0, 0)),
            tok(3 * d), tok(w), tok(w), tok(w),
            pl.BlockSpec((None, 2, TOKEN_TILE, w), lambda b, j: (b, 0, j + skip_tiles, 0)),
            tok(w), tok(w), tok(2 * w),
            vw, vw, vw, vw,
            pl.BlockSpec((None, SGU_GROUPS, SGU_CHUNK, SGU_CHUNK), lambda b, j: (layer, 0, 0, 0)),
            pl.BlockSpec((SGU_CHUNK, w), c2),
            pl.BlockSpec((None, 3, w, d), lambda b, j: (layer, 0, 0, 0), **once),
            pl.BlockSpec((None, d, d), lambda b, j: (layer, 0, 0), **once),
            vd, vd,
            pl.BlockSpec((None, d, 2 * D_FF), lambda b, j: (layer, 0, 0), **once),
            pl.BlockSpec((None, D_FF, d), lambda b, j: (layer, 0, 0), **once),
            vd, vd,
        ],
        out_specs=pl.BlockSpec((None, TOKEN_TILE, d), lambda b, j: (b, j, 0)),
        out_shape=jax.ShapeDtypeStruct((batch, nt * TOKEN_TILE, d), F32),
        scratch_shapes=[pltpu.VMEM((TOKEN_TILE, w), BF16), pltpu.VMEM((TOKEN_TILE, D_FF), BF16)],
        compiler_params=_params(("parallel", "parallel")),
        name="post",
    )(x, mods, gates, o_na, y_f, y_b, y0, g, bv, p_sgu, vecs["gn_g"], vecs["gn_b"], vecs["sgu_ln_g"],
      vecs["sgu_ln_b"], sgu_w, sgu_b_full, w_branch, w_out, vecs["ln1_g"], vecs["ln1_b"], w_gu, w_down,
      vecs["ln2_g"], vecs["ln2_b"])


def kernel(x, c, ctx, c_ctx, w_ada, b_ada, w_in, na_rpb, rwkv_mu_prev, rwkv_mu_next, rwkv_w0, rwkv_w2, rwkv_a0,
           rwkv_a2, rwkv_g2, rwkv_k_k, rwkv_k_a, rwkv_r_k, rwkv_gn_g, rwkv_gn_b, sgu_ln_g, sgu_ln_b, sgu_w, sgu_b,
           w_branch, w_out, ln1_g, ln1_b, ln2_g, ln2_b, ffn_w_gu, ffn_w_down):
    batch, seq, d = x.shape
    n_ctx = ctx.shape[1]
    depth = w_ada.shape[0]
    assert d == D_MODEL and batch <= 7
    assert n_ctx % TOKEN_TILE == 0 and seq % TOKEN_TILE == 0
    assert seq // GRID_W >= NA_WIN_H
    n_ctx_tiles = n_ctx // TOKEN_TILE
    alpha = float((2 * depth) ** 0.25)

    cond = jnp.zeros((8, d), F32).at[:batch].set(c).at[batch].set(c_ctx)
    mods = jnp.transpose(_ada_all(cond, w_ada, b_ada), (0, 2, 1, 3))

    xs = jnp.concatenate([ctx, x], axis=1)
    w_in_b = w_in.astype(BF16)
    w_branch_b = w_branch.astype(BF16)
    w_out_b = w_out.astype(BF16)
    w_gu_b = ffn_w_gu.astype(BF16)
    w_down_b = ffn_w_down.astype(BF16)
    sgu_w_b = sgu_w.astype(BF16)
    row = lambda a: a.reshape(1, -1)

    for l in range(depth):
        last = l == depth - 1
        prm = dict(
            mu_prev=row(rwkv_mu_prev[l]), mu_next=row(rwkv_mu_next[l]),
            w0=rwkv_w0[l], w2=rwkv_w2[l].reshape(2 * DECAY_LORA, BRANCH_WIDTH).astype(BF16),
            a0=rwkv_a0[l], a2=rwkv_a2[l].reshape(2 * DECAY_LORA, BRANCH_WIDTH).astype(BF16),
            g2=rwkv_g2[l].astype(BF16), k_k=row(rwkv_k_k[l]), k_a=row(rwkv_k_a[l]), r_k=row(rwkv_r_k[l]))
        gates, q, k, v, p_sgu, stage, rv, dg, g, bv = _inproj(xs, mods, w_in_b, prm, l, n_ctx_tiles)
        o_na = _na_attention(q, k, v, _na_bias_table(na_rpb[l]), n_ctx)
        q1, y0, m, n0 = _rwkv_prep(stage, rv, dg)
        y_f, y_b = _rwkv_chain(q1, m, n0, n_ctx)
        b_full = jnp.repeat(jnp.transpose(sgu_b[l]), BRANCH_WIDTH // SGU_GROUPS, axis=1)
        vecs = dict(gn_g=row(rwkv_gn_g[l]), gn_b=row(rwkv_gn_b[l]), sgu_ln_g=row(sgu_ln_g[l]),
                    sgu_ln_b=row(sgu_ln_b[l]), ln1_g=row(ln1_g[l]), ln1_b=row(ln1_b[l]), ln2_g=row(ln2_g[l]),
                    ln2_b=row(ln2_b[l]))
        xs = _post(xs, mods, gates, o_na, y_f, y_b, y0, g, bv, p_sgu, vecs, sgu_w_b, b_full, w_branch_b, w_out_b,
                   w_gu_b, w_down_b, l, n_ctx_tiles, alpha, n_ctx_tiles if last else 0)
    return xs
```
